```python
import jax, jax.numpy as jnp
from jax import lax
import numpy as np

D_MODEL = 1024
BATCH = 16
SEQ = 2048
DEPTH = 2

N_MEM = 256
D_MIX = D_MODEL
SB_HEADS = 8
SB_HEAD_DIM = (D_MIX // 2) // SB_HEADS
SB_WIDTH = SB_HEADS * SB_HEAD_DIM
SB_BLOCK = 128
RET_HEADS = 4
RET_HEAD_DIM = (D_MIX // 2) // RET_HEADS
RET_WIDTH = RET_HEADS * RET_HEAD_DIM
RET_CHUNK = 128
ROPE_BASE = 10000.0
IN_COLS = 3 * SB_WIDTH + 4 * RET_WIDTH
XA_HEADS = 4
XA_HEAD_DIM = D_MODEL // XA_HEADS
PEER_HEADS = 8
PEER_NKEYS = 128
PEER_N_EXPERTS = PEER_NKEYS * PEER_NKEYS
PEER_DQ = 256
PEER_TOPK = 16
PEER_CHUNK = 128
EPS = 1e-6

kernel_name = "hybrid_sb_retention_peer_trunk"


def rmsnorm(x, g):
    xf = x.astype(jnp.float32)
    y = xf * lax.rsqrt(jnp.mean(xf * xf, axis=-1, keepdims=True) + EPS)
    return (y * g.astype(jnp.float32)).astype(x.dtype)


def head_rmsnorm(o, g):
    H, d = o.shape[1], o.shape[3]
    of = o.astype(jnp.float32)
    y = of * lax.rsqrt(jnp.mean(of * of, axis=-1, keepdims=True) + EPS)
    return (y * g.reshape(1, H, 1, d).astype(jnp.float32)).astype(o.dtype)


def head_groupnorm(o, g):
    H, d = o.shape[1], o.shape[3]
    of = o.astype(jnp.float32)
    mu = jnp.mean(of, axis=-1, keepdims=True)
    var = jnp.mean((of - mu) ** 2, axis=-1, keepdims=True)
    y = (of - mu) * lax.rsqrt(var + EPS)
    return (y * g.reshape(1, H, 1, d).astype(jnp.float32)).astype(o.dtype)


def split_heads(t, n_heads):
    B, S, _ = t.shape
    return t.reshape(B, S, n_heads, -1).transpose(0, 2, 1, 3)


def merge_heads(t):
    B, H, S, d = t.shape
    return t.transpose(0, 2, 1, 3).reshape(B, S, H * d)


def rotary(x):
    S, d = x.shape[2], x.shape[3]
    inv = ROPE_BASE ** (-jnp.arange(0, d, 2, dtype=jnp.float32) / d)
    ang = jnp.arange(S, dtype=jnp.float32)[:, None] * inv[None, :]
    cos, sin = jnp.cos(ang), jnp.sin(ang)
    xf = x.astype(jnp.float32)
    x1, x2 = xf[..., : d // 2], xf[..., d // 2:]
    return jnp.concatenate([x1 * cos - x2 * sin, x1 * sin + x2 * cos], axis=-1).astype(x.dtype)


def stick_breaking(q, k, v):
    S, d = q.shape[2], q.shape[3]
    scale = d ** -0.5
    outs = []
    for blk in range(S // SB_BLOCK):
        t0, t1 = blk * SB_BLOCK, (blk + 1) * SB_BLOCK
        z = jnp.einsum('bhtd,bhsd->bhts', q[:, :, t0:t1], k[:, :, :t1]).astype(jnp.float32) * scale
        causal = jnp.arange(t1)[None, :] < jnp.arange(t0, t1)[:, None]
        log_not_b = jnp.where(causal, jax.nn.log_sigmoid(-z), 0.0)
        stick = lax.cumsum(log_not_b, axis=3, reverse=True) - log_not_b
        a = jnp.where(causal, jnp.exp(jax.nn.log_sigmoid(z) + stick), 0.0)
        outs.append(jnp.einsum('bhts,bhsd->bhtd', a.astype(v.dtype), v[:, :, :t1]))
    return jnp.concatenate(outs, axis=2)


def retention(q, k, v):
    B, H, S, d = q.shape
    C = RET_CHUNK
    N = S // C
    log_g = jnp.log1p(-jnp.power(2.0, -5.0 - jnp.arange(H, dtype=jnp.float32)))
    qc = q.astype(jnp.float32).reshape(B, H, N, C, d)
    kc = (k.astype(jnp.float32) * d ** -0.5).reshape(B, H, N, C, d)
    vc = v.astype(jnp.float32).reshape(B, H, N, C, v.shape[3])
    idx = jnp.arange(C, dtype=jnp.float32)
    diff = idx[:, None] - idx[None, :]
    decay = jnp.where(diff >= 0, jnp.exp(diff[None] * log_g[:, None, None]), 0.0)
    inner_s = jnp.einsum('bhncd,bhnmd->bhncm', qc, kc) * decay[None, :, None]
    inner = jnp.einsum('bhncm,bhnme->bhnce', inner_s, vc)
    zeta = jnp.exp((C - 1 - idx)[None, :] * log_g[:, None])
    xi = jnp.exp((idx + 1)[None, :] * log_g[:, None])
    chunk_kv = jnp.einsum('bhncd,bhnce->nbhde', kc * zeta[None, :, None, :, None], vc)
    decay_c = jnp.exp(C * log_g)[None, :, None, None]

    def step(state, kv_n):
        return decay_c * state + kv_n, state

    _, prev = lax.scan(step, jnp.zeros_like(chunk_kv[0]), chunk_kv)
    cross = jnp.einsum('bhncd,nbhde->bhnce', qc * xi[None, :, None, :, None], prev)
    return (inner + cross).reshape(B, H, S, v.shape[3]).astype(q.dtype)


def head_group_mixer(h, w_in, sb_norm, ret_norm, w_out):
    proj = h @ w_in
    cuts = [SB_WIDTH, 2 * SB_WIDTH, 3 * SB_WIDTH, 3 * SB_WIDTH + RET_WIDTH,
            3 * SB_WIDTH + 2 * RET_WIDTH, 3 * SB_WIDTH + 3 * RET_WIDTH]
    sb_q, sb_k, sb_v, r_q, r_k, r_v, r_g = jnp.split(proj, cuts, axis=-1)
    sb_o = stick_breaking(split_heads(sb_q, SB_HEADS), split_heads(sb_k, SB_HEADS),
                          split_heads(sb_v, SB_HEADS))
    sb_o = merge_heads(head_rmsnorm(sb_o, sb_norm))
    ret_o = retention(rotary(split_heads(r_q, RET_HEADS)), rotary(split_heads(r_k, RET_HEADS)),
                      split_heads(r_v, RET_HEADS))
    ret_o = merge_heads(head_groupnorm(ret_o, ret_norm)) * jax.nn.silu(r_g)
    return jnp.concatenate([sb_o, ret_o], axis=-1) @ w_out


def memory_cross_attn(h, mem_h, wq, wkv, wo):
    B, S, D = h.shape
    M = mem_h.shape[1]
    q = (h @ wq).reshape(B, S, XA_HEADS, XA_HEAD_DIM)
    k, v = jnp.split(mem_h @ wkv, 2, axis=-1)
    k = k.reshape(B, M, XA_HEADS, XA_HEAD_DIM)
    v = v.reshape(B, M, XA_HEADS, XA_HEAD_DIM)
    s = jnp.einsum('bshd,bmhd->bhsm', q, k).astype(jnp.float32) * XA_HEAD_DIM ** -0.5
    p = jax.nn.softmax(s, axis=-1).astype(v.dtype)
    o = jnp.einsum('bhsm,bmhd->bshd', p, v).reshape(B, S, D)
    return o @ wo


def peer_ffn(h, w_pq, sub_keys, expert_u, expert_v):
    B, S, D = h.shape
    K, NK = PEER_TOPK, PEER_NKEYS
    q = (h @ w_pq).reshape(B, S, PEER_HEADS, 2, PEER_DQ // 2)
    scores = jnp.einsum('bshpd,hpkd->bshpk', q, sub_keys).astype(jnp.float32)
    top_s, top_i = lax.top_k(scores, K)
    cand_s = (top_s[..., 0, :, None] + top_s[..., 1, None, :]).reshape(B, S, PEER_HEADS, K * K)
    cand_i = (top_i[..., 0, :, None] * NK + top_i[..., 1, None, :]).reshape(B, S, PEER_HEADS, K * K)
    best_s, best_pos = lax.top_k(cand_s, K)
    expert_idx = jnp.take_along_axis(cand_i, best_pos, axis=-1)
    gate = jax.nn.softmax(best_s, axis=-1).astype(h.dtype)
    T = B * S
    n_chunks = T // PEER_CHUNK
    h_c = h.reshape(n_chunks, PEER_CHUNK, D)
    i_c = expert_idx.reshape(n_chunks, PEER_CHUNK, PEER_HEADS, K)
    g_c = gate.reshape(n_chunks, PEER_CHUNK, PEER_HEADS, K)

    def retrieve(args):
        hc, ic, gc = args
        u = jnp.take(expert_u, ic, axis=0)
        a = jnp.einsum('cd,chkd->chk', hc, u)
        w = gc * jax.nn.gelu(a, approximate=False)
        return jnp.einsum('chk,chkd->cd', w, jnp.take(expert_v, ic, axis=0))

    return lax.map(retrieve, (h_c, i_c, g_c)).reshape(B, S, D)


def setup_inputs(seed: int = 0) -> dict:
    key = jax.random.key(seed)
    ks = jax.random.split(key, 20)
    f32 = jnp.float32

    def nrm(k, shape, scale):
        return jax.random.normal(k, shape, f32) * scale

    def gain(k, shape):
        return 1.0 + 0.01 * jax.random.normal(k, shape, f32)

    L, D = DEPTH, D_MODEL
    return {
        "x": nrm(ks[0], (BATCH, SEQ, D), 1.0),
        "mem": nrm(ks[1], (BATCH, N_MEM, D), 1.0),
        "mix_norm": gain(ks[2], (L, D)),
        "w_in": nrm(ks[3], (L, D, IN_COLS), D ** -0.5),
        "sb_norm": gain(ks[4], (L, SB_WIDTH)),
        "ret_norm": gain(ks[5], (L, RET_WIDTH)),
        "w_out": nrm(ks[6], (L, D_MIX, D), 0.5 * D_MIX ** -0.5),
        "xa_norm": gain(ks[7], (L, D)),
        "mem_norm": gain(ks[8], (L, D)),
        "xa_wq": nrm(ks[9], (L, D, D), D ** -0.5),
        "xa_wkv": nrm(ks[10], (L, D, 2 * D), D ** -0.5),
        "xa_wo": nrm(ks[11], (L, D, D), 0.5 * D ** -0.5),
        "peer_norm": gain(ks[12], (L, D)),
        "peer_wq": nrm(ks[13], (L, D, PEER_HEADS * PEER_DQ), D ** -0.5),
        "peer_keys": nrm(ks[14], (L, PEER_HEADS, 2, PEER_NKEYS, PEER_DQ // 2), (PEER_DQ // 2) ** -0.5),
        "peer_u": nrm(ks[15], (L, PEER_N_EXPERTS, D), D ** -0.5),
        "peer_v": nrm(ks[16], (L, PEER_N_EXPERTS, D), 0.25),
        "final_norm": gain(ks[17], (D,)),
    }


def reference(x, mem, mix_norm, w_in, sb_norm, ret_norm, w_out, xa_norm, mem_norm,
              xa_wq, xa_wkv, xa_wo, peer_norm, peer_wq, peer_keys, peer_u, peer_v, final_norm):
    for l in range(DEPTH):
        x = x + head_group_mixer(rmsnorm(x, mix_norm[l]), w_in[l], sb_norm[l], ret_norm[l], w_out[l])
        x = x + memory_cross_attn(rmsnorm(x, xa_norm[l]), rmsnorm(mem, mem_norm[l]),
                                  xa_wq[l], xa_wkv[l], xa_wo[l])
        x = x + peer_ffn(rmsnorm(x, peer_norm[l]), peer_wq[l], peer_keys[l], peer_u[l], peer_v[l])
    return rmsnorm(x, final_norm)
```

```python
import functools
import math

import jax
import jax.numpy as jnp
from jax import lax
from jax.experimental import pallas as pl
from jax.experimental.pallas import tpu as pltpu

F32 = jnp.float32
BF16 = jnp.bfloat16

D_MODEL = 1024
SB_HEADS = 8
SB_HEAD_DIM = 64
SB_WIDTH = 512
SB_BLOCK = 128
RET_HEADS = 4
RET_HEAD_DIM = 128
RET_WIDTH = 512
RET_CHUNK = 128
ROPE_BASE = 10000.0
IN_COLS = 3 * SB_WIDTH + 4 * RET_WIDTH
XA_HEADS = 4
XA_HEAD_DIM = 256
PEER_HEADS = 8
PEER_NKEYS = 128
PEER_DQ = 256
PEER_TOPK = 16
EPS = 1e-6

LANES = 128
SUBLANES = 8
VMEM_LIMIT = 56 * 1024 * 1024
NEG_INF = float("-inf")
EXP_ZERO_BELOW = -104.0


def _params(*sem):
    return pltpu.CompilerParams(dimension_semantics=sem, vmem_limit_bytes=VMEM_LIMIT)


def _dot(a, b):
    return jnp.dot(a, b, preferred_element_type=F32)


def _dot_nt(a, b):
    return lax.dot_general(a, b, (((1,), (1,)), ((), ())), preferred_element_type=F32)


def _rmsnorm_rows(x, g):
    ms = jnp.mean(x * x, axis=-1, keepdims=True)
    return x * lax.rsqrt(ms + EPS) * g


def _norm_matmul_kernel(x_ref, g_ref, w_ref, o_ref, h_scr):
    @pl.when(pl.program_id(1) == 0)
    def _():
        h_scr[...] = _rmsnorm_rows(x_ref[...], g_ref[...]).astype(BF16)

    o_ref[...] = _dot(h_scr[...], w_ref[...]).astype(o_ref.dtype)


def norm_matmul(x, g, w, *, tm, tn, out_dtype=BF16):
    T, D = x.shape
    N = w.shape[1]
    return pl.pallas_call(
        _norm_matmul_kernel,
        grid=(T // tm, N // tn),
        in_specs=[
            pl.BlockSpec((tm, D), lambda i, j: (i, 0)),
            pl.BlockSpec((1, D), lambda i, j: (0, 0)),
            pl.BlockSpec((D, tn), lambda i, j: (0, j)),
        ],
        out_specs=pl.BlockSpec((tm, tn), lambda i, j: (i, j)),
        out_shape=jax.ShapeDtypeStruct((T, N), out_dtype),
        scratch_shapes=[pltpu.VMEM((tm, D), BF16)],
        compiler_params=_params("parallel", "arbitrary"),
        name="norm_matmul",
    )(x, g, w)


def _softplus(z):
    return jnp.maximum(z, 0.0) + jnp.log1p(jnp.exp(-jnp.abs(z)))


def _sb_kernel(q_ref, k_ref, v_ref, g_ref, o_ref):
    blk = SB_BLOCK
    i = pl.program_id(2)
    scale = SB_HEAD_DIM ** -0.5
    lane = lax.broadcasted_iota(jnp.int32, (1, LANES), 1)
    row = lax.broadcasted_iota(jnp.int32, (blk, blk), 0)
    col = lax.broadcasted_iota(jnp.int32, (blk, blk), 1)
    causal = col < row
    r2 = lax.broadcasted_iota(jnp.int32, (2 * blk, 2 * blk), 0) % blk
    c2 = lax.broadcasted_iota(jnp.int32, (2 * blk, 2 * blk), 1)
    suffix = jnp.where((c2 >= blk) | (r2 > c2), 1.0, 0.0).astype(BF16)
    q = q_ref[...]

    def block_terms(qh, j, masked):
        ks = k_ref[pl.ds(pl.multiple_of(j * blk, blk), blk), :]
        vs = v_ref[pl.ds(pl.multiple_of(j * blk, blk), blk), :]
        z = _dot_nt(qh, ks) * scale
        sp = _softplus(z)
        lnb = jnp.where(causal, -sp, 0.0) if masked else -sp
        hi = lnb.astype(BF16)
        lo = (lnb - hi.astype(F32)).astype(BF16)
        sums = _dot(jnp.concatenate([hi, lo], axis=1), suffix)
        return z, sp, sums[:, :blk], sums[:, blk:], vs

    def head_out(hh):
        qh = jnp.where((lane // SB_HEAD_DIM) == hh, q, jnp.zeros_like(q))
        z, sp, stick, tot, vs = block_terms(qh, i, True)
        a = jnp.where(causal, jnp.exp(z - sp + stick), 0.0)
        acc = _dot(a.astype(BF16), vs)

        def cond(carry):
            jj, run, _ = carry
            return (jj <= i) & (jnp.max(run) > EXP_ZERO_BELOW)

        def body(carry):
            jj, run, acc = carry
            z, sp, stick, tot, vs = block_terms(qh, i - jj, False)
            a = jnp.exp(z - sp + stick + run)
            return jj + 1, run + tot, acc + _dot(a.astype(BF16), vs)

        _, _, acc = lax.while_loop(cond, body, (jnp.int32(1), tot, acc))
        return acc

    first = lane < SB_HEAD_DIM
    o = jnp.where(first, head_out(0), head_out(1))
    sq = o * o
    ms0 = jnp.sum(jnp.where(first, sq, 0.0), axis=-1, keepdims=True)
    ms1 = jnp.sum(jnp.where(first, 0.0, sq), axis=-1, keepdims=True)
    ms = jnp.where(first, ms0, ms1) * (1.0 / SB_HEAD_DIM)
    o_ref[...] = (o * lax.rsqrt(ms + EPS) * g_ref[...]).astype(o_ref.dtype)


def sb_attention(proj, sb_norm):
    B, S, _ = proj.shape
    pairs = SB_WIDTH // LANES
    return pl.pallas_call(
        _sb_kernel,
        grid=(B, pairs, S // SB_BLOCK),
        in_specs=[
            pl.BlockSpec((None, SB_BLOCK, LANES), lambda b, p, i: (b, i, p)),
            pl.BlockSpec((None, S, LANES), lambda b, p, i: (b, 0, pairs + p)),
            pl.BlockSpec((None, S, LANES), lambda b, p, i: (b, 0, 2 * pairs + p)),
            pl.BlockSpec((1, LANES), lambda b, p, i: (0, p)),
        ],
        out_specs=pl.BlockSpec((None, SB_BLOCK, LANES), lambda b, p, i: (b, i, p)),
        out_shape=jax.ShapeDtypeStruct((B, S, SB_WIDTH), BF16),
        compiler_params=_params("parallel", "parallel", "arbitrary"),
        name="sb_attention",
    )(proj, proj, proj, sb_norm)


def _retention_kernel(q_ref, k_ref, v_ref, gate_ref, cos_ref, sin_ref, lg_ref, g_ref, o_ref, state):
    C = RET_CHUNK
    n = pl.program_id(2)

    @pl.when(n == 0)
    def _():
        state[...] = jnp.zeros_like(state)

    lg = lg_ref[...]
    cos, sin = cos_ref[...], sin_ref[...]

    def rope(t):
        t = t.astype(F32)
        return t * cos + pltpu.roll(t, RET_HEAD_DIM // 2, 1) * sin

    row = lax.broadcasted_iota(jnp.int32, (C, C), 0).astype(F32)
    col = lax.broadcasted_iota(jnp.int32, (C, C), 1).astype(F32)
    diff = row - col
    decay = jnp.where(diff >= 0, jnp.exp(diff * lg), 0.0)
    zeta = jnp.exp((C - 1 - row) * lg)
    xi = jnp.exp((row + 1) * lg)
    decay_c = jnp.exp(C * lg)

    qc = rope(q_ref[...])
    kc = rope(k_ref[...]) * RET_HEAD_DIM ** -0.5
    vc = v_ref[...]
    st = state[...]
    inner_s = _dot_nt(qc.astype(BF16), kc.astype(BF16)) * decay
    o = _dot(inner_s.astype(BF16), vc) + _dot((qc * xi).astype(BF16), st.astype(BF16))
    state[...] = decay_c * st + _dot((kc * zeta).T.astype(BF16), vc)

    mu = jnp.mean(o, axis=-1, keepdims=True)
    var = jnp.mean((o - mu) ** 2, axis=-1, keepdims=True)
    y = (o - mu) * lax.rsqrt(var + EPS) * g_ref[...]
    gate = gate_ref[...].astype(F32)
    o_ref[...] = (y * (gate * jax.nn.sigmoid(gate))).astype(o_ref.dtype)


def retention(proj, ret_norm, cos, sin, log_gamma):
    B, S, _ = proj.shape
    C = RET_CHUNK
    base = 3 * SB_WIDTH // LANES

    def col(group):
        return pl.BlockSpec((None, C, LANES), lambda b, h, n: (b, n, base + group * RET_HEADS + h))

    return pl.pallas_call(
        _retention_kernel,
        grid=(B, RET_HEADS, S // C),
        in_specs=[
            col(0), col(1), col(2), col(3),
            pl.BlockSpec((C, LANES), lambda b, h, n: (n, 0)),
            pl.BlockSpec((C, LANES), lambda b, h, n: (n, 0)),
            pl.BlockSpec((None, 1, LANES), lambda b, h, n: (h, 0, 0)),
            pl.BlockSpec((1, LANES), lambda b, h, n: (0, h)),
        ],
        out_specs=pl.BlockSpec((None, C, LANES), lambda b, h, n: (b, n, h)),
        out_shape=jax.ShapeDtypeStruct((B, S, RET_WIDTH), BF16),
        scratch_shapes=[pltpu.VMEM((RET_HEAD_DIM, RET_HEAD_DIM), F32)],
        compiler_params=_params("parallel", "parallel", "arbitrary"),
        name="retention",
    )(proj, proj, proj, proj, cos, sin, log_gamma, ret_norm)


def _out_proj_kernel(x_ref, a_ref, b_ref, wa_ref, wb_ref, o_ref):
    o_ref[...] = x_ref[...] + _dot(a_ref[...], wa_ref[...]) + _dot(b_ref[...], wb_ref[...])


def out_proj(x, a, b, wa, wb, *, tm):
    T, D = x.shape
    row = lambda i: (i, 0)
    const = lambda i: (0, 0)
    return pl.pallas_call(
        _out_proj_kernel,
        grid=(T // tm,),
        in_specs=[
            pl.BlockSpec((tm, D), row),
            pl.BlockSpec((tm, a.shape[1]), row),
            pl.BlockSpec((tm, b.shape[1]), row),
            pl.BlockSpec(wa.shape, const),
            pl.BlockSpec(wb.shape, const),
        ],
        out_specs=pl.BlockSpec((tm, D), row),
        out_shape=jax.ShapeDtypeStruct((T, D), F32),
        compiler_params=_params("parallel"),
        name="out_proj",
    )(x, a, b, wa, wb)


def _cross_attn_kernel(x_ref, g_ref, wq_ref, kv_ref, wo_ref, o_ref):
    x = x_ref[...]
    h = _rmsnorm_rows(x, g_ref[...]).astype(BF16)
    q = _dot(h, wq_ref[...]).astype(BF16)
    outs = []
    for hd in range(XA_HEADS):
        lo = hd * XA_HEAD_DIM
        k = kv_ref[:, lo:lo + XA_HEAD_DIM]
        v = kv_ref[:, D_MODEL + lo:D_MODEL + lo + XA_HEAD_DIM]
        s = _dot_nt(q[:, lo:lo + XA_HEAD_DIM], k) * XA_HEAD_DIM ** -0.5
        e = jnp.exp(s - jnp.max(s, axis=-1, keepdims=True))
        p = e / jnp.sum(e, axis=-1, keepdims=True)
        outs.append(_dot(p.astype(BF16), v).astype(BF16))
    o_ref[...] = x + _dot(jnp.concatenate(outs, axis=1), wo_ref[...])


def cross_attn(x, g, wq, kv, wo, *, tq):
    B, S, D = x.shape
    M = kv.shape[1]
    const = lambda b, i: (0, 0)
    return pl.pallas_call(
        _cross_attn_kernel,
        grid=(B, S // tq),
        in_specs=[
            pl.BlockSpec((None, tq, D), lambda b, i: (b, i, 0)),
            pl.BlockSpec((1, D), const),
            pl.BlockSpec((D, D), const),
            pl.BlockSpec((None, M, 2 * D), lambda b, i: (b, 0, 0)),
            pl.BlockSpec((D, D), const),
        ],
        out_specs=pl.BlockSpec((None, tq, D), lambda b, i: (b, i, 0)),
        out_shape=jax.ShapeDtypeStruct((B, S, D), F32),
        compiler_params=_params("parallel", "arbitrary"),
        name="cross_attn",
    )(x, g, wq, kv, wo)


def _top_values(s, k):
    rowid = lax.broadcasted_iota(jnp.int32, (k, s.shape[1]), 0)
    vals, stacked = [], jnp.full((k, s.shape[1]), NEG_INF, F32)
    for r in range(k):
        m = jnp.max(s, axis=0, keepdims=True)
        vals.append(m)
        stacked = jnp.where(rowid == r, m, stacked)
        if r + 1 < k:
            s = jnp.where(s == m, NEG_INF, s)
    return vals, stacked


def _pair_candidates(t1, t1s, t2, t2s):
    K, sub = PEER_TOPK, 8
    rowid = lax.broadcasted_iota(jnp.int32, (sub, t1s.shape[1]), 0)
    pieces = [t1[0] + t2s]
    for a in range(1, sub):
        pieces.append(jnp.where(rowid < K // (a + 1), t1[a] + t2s[:sub], NEG_INF))
    pieces.append(t1s[sub:] + t2[0])
    return jnp.concatenate(pieces, axis=0)


def _peer_select_kernel(x_ref, g_ref, wq_ref, keys_ref, hn_ref, s1_ref, e1_ref, s2_ref, e2_ref, tau_ref):
    K = PEER_TOPK
    h = _rmsnorm_rows(x_ref[...], g_ref[...]).astype(BF16)
    hn_ref[...] = h
    q = _dot(h, wq_ref[...]).astype(BF16)
    half = PEER_DQ // 2
    for hd in range(PEER_HEADS):
        s1 = _dot_nt(keys_ref[hd, 0], q[:, hd * PEER_DQ:hd * PEER_DQ + half])
        s2 = _dot_nt(keys_ref[hd, 1], q[:, hd * PEER_DQ + half:(hd + 1) * PEER_DQ])
        t1, t1s = _top_values(s1, K)
        t2, t2s = _top_values(s2, K)
        cand = _pair_candidates(t1, t1s, t2, t2s)
        tau = _top_values(cand, K)[0][K - 1]
        z = jnp.sum(jnp.where(cand >= tau, jnp.exp(cand - (t1[0] + t2[0])), 0.0), axis=0, keepdims=True)
        s1_ref[hd] = s1
        e1_ref[hd] = jnp.where(s1 >= t1[K - 1], jnp.exp(s1 - t1[0]), 0.0)
        s2_ref[hd] = jnp.where(s2 >= t2[K - 1], s2, NEG_INF)
        e2_ref[hd] = jnp.exp(s2 - t2[0]) / z
        tau_ref[pl.ds(hd, 1), :] = tau


def peer_select(x, g, wq, keys, *, tt):
    T, D = x.shape
    H, NK = PEER_HEADS, PEER_NKEYS
    const2 = lambda i: (0, 0)
    sel_spec = pl.BlockSpec((H, NK, tt), lambda i: (0, 0, i))
    sel_shape = jax.ShapeDtypeStruct((H, NK, T), F32)
    return pl.pallas_call(
        _peer_select_kernel,
        grid=(T // tt,),
        in_specs=[
            pl.BlockSpec((tt, D), lambda i: (i, 0)),
            pl.BlockSpec((1, D), const2),
            pl.BlockSpec(wq.shape, const2),
            pl.BlockSpec(keys.shape, lambda i: (0, 0, 0, 0)),
        ],
        out_specs=[
            pl.BlockSpec((tt, D), lambda i: (i, 0)),
            sel_spec, sel_spec, sel_spec, sel_spec,
            pl.BlockSpec((H, tt), lambda i: (0, i)),
        ],
        out_shape=[
            jax.ShapeDtypeStruct((T, D), BF16),
            sel_shape, sel_shape, sel_shape, sel_shape,
            jax.ShapeDtypeStruct((H, T), F32),
        ],
        compiler_params=_params("parallel"),
        name="peer_select",
    )(x, g, wq, keys)


def _gelu(a):
    return a * 0.5 * (1.0 + lax.erf(a * (1.0 / math.sqrt(2.0))))


def _peer_dense_kernel(x_ref, hn_ref, u_ref, vt_ref, s1_ref, e1_ref, s2_ref, e2_ref, tau_ref,
                       o_ref, acc, w_scr, *, groups):
    j = pl.program_id(1)
    tt = hn_ref.shape[0]

    @pl.when(j == 0)
    def _():
        acc[...] = jnp.zeros_like(acc)

    hn = hn_ref[...]

    def group(cc, carry):
        c = j * groups + cc
        rows = pl.ds(pl.multiple_of(cc * LANES, LANES), LANES)
        a = _dot_nt(u_ref[rows, :], hn)
        c8 = pl.multiple_of((c // SUBLANES) * SUBLANES, SUBLANES)
        back = (SUBLANES - c % SUBLANES) % SUBLANES

        def row_of(ref, hd, tok):
            return pltpu.roll(ref[hd, pl.ds(c8, SUBLANES), tok], back, 0)[:1]

        for t0 in range(0, tt, LANES):
            tok = slice(t0, t0 + LANES)
            gate = jnp.zeros((PEER_NKEYS, LANES), F32)
            for hd in range(PEER_HEADS):
                s1r = row_of(s1_ref, hd, tok)
                e1r = row_of(e1_ref, hd, tok)
                hit = (s2_ref[hd, :, tok] + s1r) >= tau_ref[pl.ds(hd, 1), tok]
                gate = gate + jnp.where(hit, e2_ref[hd, :, tok] * e1r, 0.0)
            w_scr[rows, tok] = (gate * _gelu(a[:, tok])).astype(BF16)
        return carry

    lax.fori_loop(0, groups, group, 0)
    acc[...] += _dot(vt_ref[...], w_scr[...])

    @pl.when(j == pl.num_programs(1) - 1)
    def _():
        o_ref[...] = x_ref[...] + acc[...].T


def peer_dense(x, hn, u, vt, s1, e1, s2, e2, tau, *, tt, ec):
    T, D = x.shape
    E = u.shape[0]
    H, NK = PEER_HEADS, PEER_NKEYS
    sel_spec = pl.BlockSpec((H, NK, tt), lambda i, j: (0, 0, i))
    return pl.pallas_call(
        functools.partial(_peer_dense_kernel, groups=ec // NK),
        grid=(T // tt, E // ec),
        in_specs=[
            pl.BlockSpec((tt, D), lambda i, j: (i, 0)),
            pl.BlockSpec((tt, D), lambda i, j: (i, 0)),
            pl.BlockSpec((ec, D), lambda i, j: (j, 0)),
            pl.BlockSpec((D, ec), lambda i, j: (0, j)),
            sel_spec, sel_spec, sel_spec, sel_spec,
            pl.BlockSpec((H, tt), lambda i, j: (0, i)),
        ],
        out_specs=pl.BlockSpec((tt, D), lambda i, j: (i, 0)),
        out_shape=jax.ShapeDtypeStruct((T, D), F32),
        scratch_shapes=[pltpu.VMEM((D, tt), F32), pltpu.VMEM((ec, tt), BF16)],
        compiler_params=_params("parallel", "arbitrary"),
        name="peer_dense",
    )(x, hn, u, vt, s1, e1, s2, e2, tau)


def _final_norm_kernel(x_ref, g_ref, o_ref):
    o_ref[...] = _rmsnorm_rows(x_ref[...], g_ref[...])


def final_norm(x, g, *, tm):
    T, D = x.shape
    return pl.pallas_call(
        _final_norm_kernel,
        grid=(T // tm,),
        in_specs=[pl.BlockSpec((tm, D), lambda i: (i, 0)), pl.BlockSpec((1, D), lambda i: (0, 0))],
        out_specs=pl.BlockSpec((tm, D), lambda i: (i, 0)),
        out_shape=jax.ShapeDtypeStruct((T, D), F32),
        compiler_params=_params("parallel"),
        name="final_norm",
    )(x, g)


def _rope_tables(S):
    d = RET_HEAD_DIM
    inv = ROPE_BASE ** (-jnp.arange(0, d, 2, dtype=F32) / d)
    ang = jnp.arange(S, dtype=F32)[:, None] * inv[None, :]
    cos, sin = jnp.cos(ang), jnp.sin(ang)
    return jnp.concatenate([cos, cos], axis=1), jnp.concatenate([-sin, sin], axis=1)


def _tile(n, want):
    t = min(n, want)
    assert n % t == 0, (n, t)
    return t


def kernel(x, mem, mix_norm, w_in, sb_norm, ret_norm, w_out, xa_norm, mem_norm, xa_wq, xa_wkv, xa_wo,
           peer_norm, peer_wq, peer_keys, peer_u, peer_v, final_norm_g):
    B, S, D = x.shape
    M = mem.shape[1]
    T = B * S
    depth = w_in.shape[0]
    assert D == D_MODEL and S % SB_BLOCK == 0 and w_in.shape[2] == IN_COLS

    cos, sin = _rope_tables(S)
    log_gamma = jnp.log1p(-jnp.power(2.0, -5.0 - jnp.arange(RET_HEADS, dtype=F32)))
    log_gamma = jnp.broadcast_to(log_gamma[:, None, None], (RET_HEADS, 1, LANES))
    row = lambda v: v.reshape(1, -1).astype(F32)
    tm = _tile(T, 512)

    xt = x.reshape(T, D)
    mem2 = mem.reshape(B * M, D)
    for l in range(depth):
        proj = norm_matmul(xt, row(mix_norm[l]), w_in[l].astype(BF16), tm=tm, tn=IN_COLS // 7)
        proj = proj.reshape(B, S, IN_COLS)
        sb_o = sb_attention(proj, row(sb_norm[l]))
        ret_o = retention(proj, row(ret_norm[l]), cos, sin, log_gamma)
        w_o = w_out[l].astype(BF16)
        xt = out_proj(xt, sb_o.reshape(T, SB_WIDTH), ret_o.reshape(T, RET_WIDTH),
                      w_o[:SB_WIDTH], w_o[SB_WIDTH:], tm=tm)

        kv = norm_matmul(mem2, row(mem_norm[l]), xa_wkv[l].astype(BF16), tm=_tile(B * M, 512), tn=1024)
        xt = cross_attn(xt.reshape(B, S, D), row(xa_norm[l]), xa_wq[l].astype(BF16),
                        kv.reshape(B, M, 2 * D), xa_wo[l].astype(BF16), tq=_tile(S, 512)).reshape(T, D)

        tt = _tile(T, 256)
        hn, s1, e1, s2, e2, tau = peer_select(xt, row(peer_norm[l]), peer_wq[l].astype(BF16),
                                              peer_keys[l].astype(BF16), tt=tt)
        xt = peer_dense(xt, hn, peer_u[l].astype(BF16), peer_v[l].T.astype(BF16),
                        s1, e1, s2, e2, tau, tt=tt, ec=2048)
    return final_norm(xt, row(final_norm_g), tm=tm).reshape(B, S, D)
```

```python
import functools
import math

import jax
import jax.numpy as jnp
from jax import lax
from jax.experimental import pallas as pl
from jax.experimental.pallas import tpu as pltpu

F32 = jnp.float32
BF16 = jnp.bfloat16

D_MODEL = 1024
SB_HEADS = 8
SB_HEAD_DIM = 64
SB_WIDTH = 512
SB_BLOCK = 128
RET_HEADS = 4
RET_HEAD_DIM = 128
RET_WIDTH = 512
RET_CHUNK = 128
ROPE_BASE = 10000.0
IN_COLS = 3 * SB_WIDTH + 4 * RET_WIDTH
XA_HEADS = 4
XA_HEAD_DIM = 256
PEER_HEADS = 8
PEER_NKEYS = 128
PEER_DQ = 256
PEER_TOPK = 16
EPS = 1e-6

LANES = 128
SUBLANES = 8
VMEM_LIMIT = 56 * 1024 * 1024
NEG_INF = float("-inf")
EXP_ZERO_BELOW = -104.0


def _params(*sem):
    return pltpu.CompilerParams(dimension_semantics=sem, vmem_limit_bytes=VMEM_LIMIT)


def _dot(a, b):
    return jnp.dot(a, b, preferred_element_type=F32)


def _dot_nt(a, b):
    return lax.dot_general(a, b, (((1,), (1,)), ((), ())), preferred_element_type=F32)


def _rmsnorm_rows(x, g):
    ms = jnp.mean(x * x, axis=-1, keepdims=True)
    return x * lax.rsqrt(ms + EPS) * g


def _norm_matmul_kernel(x_ref, g_ref, w_ref, o_ref, h_scr):
    @pl.when(pl.program_id(1) == 0)
    def _():
        h_scr[...] = _rmsnorm_rows(x_ref[...], g_ref[...]).astype(BF16)

    o_ref[...] = _dot(h_scr[...], w_ref[...]).astype(o_ref.dtype)


def norm_matmul(x, g, w, *, tm, tn, out_dtype=BF16):
    T, D = x.shape
    N = w.shape[1]
    return pl.pallas_call(
        _norm_matmul_kernel,
        grid=(T // tm, N // tn),
        in_specs=[
            pl.BlockSpec((tm, D), lambda i, j: (i, 0)),
            pl.BlockSpec((1, D), lambda i, j: (0, 0)),
            pl.BlockSpec((D, tn), lambda i, j: (0, j)),
        ],
        out_specs=pl.BlockSpec((tm, tn), lambda i, j: (i, j)),
        out_shape=jax.ShapeDtypeStruct((T, N), out_dtype),
        scratch_shapes=[pltpu.VMEM((tm, D), BF16)],
        compiler_params=_params("parallel", "arbitrary"),
        name="norm_matmul",
    )(x, g, w)


def _softplus(z):
    return jnp.maximum(z, 0.0) + jnp.log1p(jnp.exp(-jnp.abs(z)))


def _sb_kernel(q_ref, k_ref, v_ref, g_ref, o_ref, acc, run):
    blk = SB_BLOCK
    pairs = SB_WIDTH // LANES
    i = pl.program_id(1)
    scale = SB_HEAD_DIM ** -0.5
    first = lax.broadcasted_iota(jnp.int32, (1, LANES), 1) < SB_HEAD_DIM
    row = lax.broadcasted_iota(jnp.int32, (2 * blk, blk), 0) % blk
    col = lax.broadcasted_iota(jnp.int32, (2 * blk, blk), 1)
    causal = col < row
    r2 = lax.broadcasted_iota(jnp.int32, (2 * blk, 2 * blk), 0) % blk
    c2 = lax.broadcasted_iota(jnp.int32, (2 * blk, 2 * blk), 1)
    suffix = jnp.where((c2 >= blk) | (r2 > c2), 1.0, 0.0).astype(BF16)

    def split_heads(t):
        zero = jnp.zeros_like(t)
        return jnp.concatenate([jnp.where(first, t, zero), jnp.where(first, zero, t)], axis=0)

    def add_key_block(j, diagonal):
        rows = pl.ds(pl.multiple_of(j * blk, blk), blk)
        col_of = [slice(p * LANES, (p + 1) * LANES) for p in range(pairs)]
        zs = [_dot_nt(split_heads(q_ref[:, c]), k_ref[rows, c]) * scale for c in col_of]
        sps = [_softplus(z) for z in zs]
        sums = []
        for sp in sps:
            lnb = jnp.where(causal, -sp, 0.0) if diagonal else -sp
            hi = lnb.astype(BF16)
            lo = (lnb - hi.astype(F32)).astype(BF16)
            sums.append(_dot(jnp.concatenate([hi, lo], axis=1), suffix))
        outs = []
        for p in range(pairs):
            stick, tot = sums[p][:, :blk], sums[p][:, blk:]
            if diagonal:
                a = jnp.where(causal, jnp.exp(zs[p] - sps[p] + stick), 0.0)
                run[p] = tot
            else:
                a = jnp.exp(zs[p] - sps[p] + stick + run[p])
                run[p] += tot
            a = a.astype(BF16)
            outs.append(_dot(jnp.concatenate([a[:blk], a[blk:]], axis=1),
                             split_heads(v_ref[rows, col_of[p]])))
        for p in range(pairs):
            acc[p] = outs[p] if diagonal else acc[p] + outs[p]

    def highest_run():
        m = run[0]
        for p in range(1, pairs):
            m = jnp.maximum(m, run[p])
        return jnp.max(m)

    add_key_block(i, True)

    def cond(carry):
        jj, top = carry
        return (jj <= i) & (top > EXP_ZERO_BELOW)

    def body(carry):
        jj, _ = carry
        add_key_block(i - jj, False)
        return jj + 1, highest_run()

    lax.while_loop(cond, body, (jnp.int32(1), highest_run()))

    for p in range(pairs):
        cols = slice(p * LANES, (p + 1) * LANES)
        o = acc[p]
        sq = o * o
        ms0 = jnp.sum(jnp.where(first, sq, 0.0), axis=-1, keepdims=True)
        ms1 = jnp.sum(jnp.where(first, 0.0, sq), axis=-1, keepdims=True)
        ms = jnp.where(first, ms0, ms1) * (1.0 / SB_HEAD_DIM)
        o_ref[:, cols] = (o * lax.rsqrt(ms + EPS) * g_ref[:, cols]).astype(o_ref.dtype)


def sb_attention(proj, sb_norm):
    B, S, _ = proj.shape
    pairs = SB_WIDTH // LANES
    return pl.pallas_call(
        _sb_kernel,
        grid=(B, S // SB_BLOCK),
        in_specs=[
            pl.BlockSpec((None, SB_BLOCK, SB_WIDTH), lambda b, i: (b, i, 0)),
            pl.BlockSpec((None, S, SB_WIDTH), lambda b, i: (b, 0, 1)),
            pl.BlockSpec((None, S, SB_WIDTH), lambda b, i: (b, 0, 2)),
            pl.BlockSpec((1, SB_WIDTH), lambda b, i: (0, 0)),
        ],
        out_specs=pl.BlockSpec((None, SB_BLOCK, SB_WIDTH), lambda b, i: (b, i, 0)),
        out_shape=jax.ShapeDtypeStruct((B, S, SB_WIDTH), BF16),
        scratch_shapes=[pltpu.VMEM((pairs, SB_BLOCK, LANES), F32),
                        pltpu.VMEM((pairs, 2 * SB_BLOCK, LANES), F32)],
        compiler_params=_params("parallel", "arbitrary"),
        name="sb_attention",
    )(proj, proj, proj, sb_norm)


def _retention_kernel(q_ref, k_ref, v_ref, gate_ref, cos_ref, sin_ref, lg_ref, g_ref, o_ref, state):
    C = RET_CHUNK
    n = pl.program_id(1)

    @pl.when(n == 0)
    def _():
        state[...] = jnp.zeros_like(state)

    cos, sin = cos_ref[...], sin_ref[...]

    def rope(t):
        t = t.astype(F32)
        return t * cos + pltpu.roll(t, RET_HEAD_DIM // 2, 1) * sin

    row = lax.broadcasted_iota(jnp.int32, (C, C), 0).astype(F32)
    col = lax.broadcasted_iota(jnp.int32, (C, C), 1).astype(F32)
    diff = row - col
    for hd in range(RET_HEADS):
        cols = slice(hd * RET_HEAD_DIM, (hd + 1) * RET_HEAD_DIM)
        lg = lg_ref[hd]
        decay = jnp.where(diff >= 0, jnp.exp(diff * lg), 0.0)
        zeta = jnp.exp((C - 1 - row) * lg)
        xi = jnp.exp((row + 1) * lg)
        decay_c = jnp.exp(C * lg)

        qc = rope(q_ref[:, cols])
        kc = rope(k_ref[:, cols]) * RET_HEAD_DIM ** -0.5
        vc = v_ref[:, cols]
        st = state[hd]
        inner_s = _dot_nt(qc.astype(BF16), kc.astype(BF16)) * decay
        o = _dot(inner_s.astype(BF16), vc) + _dot((qc * xi).astype(BF16), st.astype(BF16))
        state[hd] = decay_c * st + _dot((kc * zeta).T.astype(BF16), vc)

        mu = jnp.mean(o, axis=-1, keepdims=True)
        var = jnp.mean((o - mu) ** 2, axis=-1, keepdims=True)
        y = (o - mu) * lax.rsqrt(var + EPS) * g_ref[:, cols]
        gate = gate_ref[:, cols].astype(F32)
        o_ref[:, cols] = (y * (gate * jax.nn.sigmoid(gate))).astype(o_ref.dtype)


def retention(proj, ret_norm, cos, sin, log_gamma):
    B, S, _ = proj.shape
    C = RET_CHUNK
    base = 3 * SB_WIDTH // RET_WIDTH

    def col(group):
        return pl.BlockSpec((None, C, RET_WIDTH), lambda b, n: (b, n, base + group))

    return pl.pallas_call(
        _retention_kernel,
        grid=(B, S // C),
        in_specs=[
            col(0), col(1), col(2), col(3),
            pl.BlockSpec((C, LANES), lambda b, n: (n, 0)),
            pl.BlockSpec((C, LANES), lambda b, n: (n, 0)),
            pl.BlockSpec((RET_HEADS, 1, LANES), lambda b, n: (0, 0, 0)),
            pl.BlockSpec((1, RET_WIDTH), lambda b, n: (0, 0)),
        ],
        out_specs=pl.BlockSpec((None, C, RET_WIDTH), lambda b, n: (b, n, 0)),
        out_shape=jax.ShapeDtypeStruct((B, S, RET_WIDTH), BF16),
        scratch_shapes=[pltpu.VMEM((RET_HEADS, RET_HEAD_DIM, RET_HEAD_DIM), F32)],
        compiler_params=_params("parallel", "arbitrary"),
        name="retention",
    )(proj, proj, proj, proj, cos, sin, log_gamma, ret_norm)


def _out_proj_kernel(x_ref, a_ref, b_ref, wa_ref, wb_ref, o_ref):
    o_ref[...] = x_ref[...] + _dot(a_ref[...], wa_ref[...]) + _dot(b_ref[...], wb_ref[...])


def out_proj(x, a, b, wa, wb, *, tm):
    T, D = x.shape
    row = lambda i: (i, 0)
    const = lambda i: (0, 0)
    return pl.pallas_call(
        _out_proj_kernel,
        grid=(T // tm,),
        in_specs=[
            pl.BlockSpec((tm, D), row),
            pl.BlockSpec((tm, a.shape[1]), row),
            pl.BlockSpec((tm, b.shape[1]), row),
            pl.BlockSpec(wa.shape, const),
            pl.BlockSpec(wb.shape, const),
        ],
        out_specs=pl.BlockSpec((tm, D), row),
        out_shape=jax.ShapeDtypeStruct((T, D), F32),
        compiler_params=_params("parallel"),
        name="out_proj",
    )(x, a, b, wa, wb)


def _cross_attn_kernel(x_ref, g_ref, wq_ref, kv_ref, wo_ref, o_ref):
    x = x_ref[...]
    h = _rmsnorm_rows(x, g_ref[...]).astype(BF16)
    q = _dot(h, wq_ref[...]).astype(BF16)
    outs = []
    for hd in range(XA_HEADS):
        lo = hd * XA_HEAD_DIM
        k = kv_ref[:, lo:lo + XA_HEAD_DIM]
        v = kv_ref[:, D_MODEL + lo:D_MODEL + lo + XA_HEAD_DIM]
        s = _dot_nt(q[:, lo:lo + XA_HEAD_DIM], k) * XA_HEAD_DIM ** -0.5
        e = jnp.exp(s - jnp.max(s, axis=-1, keepdims=True))
        p = e / jnp.sum(e, axis=-1, keepdims=True)
        outs.append(_dot(p.astype(BF16), v).astype(BF16))
    o_ref[...] = x + _dot(jnp.concatenate(outs, axis=1), wo_ref[...])


def cross_attn(x, g, wq, kv, wo, *, tq):
    B, S, D = x.shape
    M = kv.shape[1]
    const = lambda b, i: (0, 0)
    return pl.pallas_call(
        _cross_attn_kernel,
        grid=(B, S // tq),
        in_specs=[
            pl.BlockSpec((None, tq, D), lambda b, i: (b, i, 0)),
            pl.BlockSpec((1, D), const),
            pl.BlockSpec((D, D), const),
            pl.BlockSpec((None, M, 2 * D), lambda b, i: (b, 0, 0)),
            pl.BlockSpec((D, D), const),
        ],
        out_specs=pl.BlockSpec((None, tq, D), lambda b, i: (b, i, 0)),
        out_shape=jax.ShapeDtypeStruct((B, S, D), F32),
        compiler_params=_params("parallel", "arbitrary"),
        name="cross_attn",
    )(x, g, wq, kv, wo)


def _kth_value(s, k):
    for _ in range(k - 1):
        s = jnp.where(s == jnp.max(s, axis=0, keepdims=True), NEG_INF, s)
    return jnp.max(s, axis=0, keepdims=True)


def _top_values(s, k):
    rowid = lax.broadcasted_iota(jnp.int32, (k, s.shape[1]), 0)
    vals, stacked = [], jnp.full((k, s.shape[1]), NEG_INF, F32)
    rank = jnp.full(s.shape, float(k), F32)
    for r in range(k):
        m = jnp.max(s, axis=0, keepdims=True)
        vals.append(m)
        stacked = jnp.where(rowid == r, m, stacked)
        hit = s == m
        rank = jnp.where(hit, float(r), rank)
        s = jnp.where(hit, NEG_INF, s)
    return vals, stacked, rank


def _pair_candidates(t1, t1s, t2, t2s):
    K, sub = PEER_TOPK, 8
    rowid = lax.broadcasted_iota(jnp.int32, (sub, t1s.shape[1]), 0)
    pieces = [t1[0] + t2s]
    for a in range(1, sub):
        pieces.append(jnp.where(rowid < K // (a + 1), t1[a] + t2s[:sub], NEG_INF))
    pieces.append(t1s[sub:] + t2[0])
    return jnp.concatenate(pieces, axis=0)


def _peer_select_kernel(x_ref, g_ref, wq_ref, keys_ref, hn_ref, n1_ref, e1_ref, r2_ref, e2_ref):
    K = PEER_TOPK
    h = _rmsnorm_rows(x_ref[...], g_ref[...]).astype(BF16)
    hn_ref[...] = h
    q = _dot(h, wq_ref[...]).astype(BF16)
    half = PEER_DQ // 2
    for hd in range(PEER_HEADS):
        s1 = _dot_nt(keys_ref[hd, 0], q[:, hd * PEER_DQ:hd * PEER_DQ + half])
        s2 = _dot_nt(keys_ref[hd, 1], q[:, hd * PEER_DQ + half:(hd + 1) * PEER_DQ])
        t1, t1s, rank1 = _top_values(s1, K)
        t2, t2s, rank2 = _top_values(s2, K)
        cand = _pair_candidates(t1, t1s, t2, t2s)
        tau = _kth_value(cand, K)
        z = jnp.sum(jnp.where(cand >= tau, jnp.exp(cand - (t1[0] + t2[0])), 0.0), axis=0, keepdims=True)
        n1 = jnp.zeros_like(s1)
        for a in range(K):
            n_a = jnp.sum(jnp.where(t1[a] + t2s >= tau, 1.0, 0.0), axis=0, keepdims=True)
            n1 = jnp.where(rank1 == float(a), n_a, n1)
        n1_ref[hd] = n1
        e1_ref[hd] = jnp.where(rank1 < float(K), jnp.exp(s1 - t1[0]), 0.0)
        r2_ref[hd] = rank2.astype(BF16)
        e2_ref[hd] = (jnp.exp(s2 - t2[0]) / z).astype(BF16)


def peer_select(x, g, wq, keys, *, tt):
    T, D = x.shape
    H, NK = PEER_HEADS, PEER_NKEYS
    const2 = lambda i: (0, 0)
    sel_spec = pl.BlockSpec((H, NK, tt), lambda i: (0, 0, i))
    return pl.pallas_call(
        _peer_select_kernel,
        grid=(T // tt,),
        in_specs=[
            pl.BlockSpec((tt, D), lambda i: (i, 0)),
            pl.BlockSpec((1, D), const2),
            pl.BlockSpec(wq.shape, const2),
            pl.BlockSpec(keys.shape, lambda i: (0, 0, 0, 0)),
        ],
        out_specs=[pl.BlockSpec((tt, D), lambda i: (i, 0)), sel_spec, sel_spec, sel_spec, sel_spec],
        out_shape=[
            jax.ShapeDtypeStruct((T, D), BF16),
            jax.ShapeDtypeStruct((H, NK, T), F32),
            jax.ShapeDtypeStruct((H, NK, T), F32),
            jax.ShapeDtypeStruct((H, NK, T), BF16),
            jax.ShapeDtypeStruct((H, NK, T), BF16),
        ],
        compiler_params=_params("parallel"),
        name="peer_select",
    )(x, g, wq, keys)


def _gelu(a):
    return a * 0.5 * (1.0 + lax.erf(a * (1.0 / math.sqrt(2.0))))


BF16_ROWS = 2 * SUBLANES
MXU_DIM = 256


def _peer_dense_kernel(x_ref, hn_ref, u_ref, vt_ref, n1_ref, e1_ref, r2_in, e2_in, o_ref,
                       acc, hn_scr, r2_ref, e2_ref):
    j = pl.program_id(1)
    tt = hn_ref.shape[0]
    steps = u_ref.shape[0] // MXU_DIM

    @pl.when(j == 0)
    def _():
        acc[...] = jnp.zeros_like(acc)
        hn_scr[...] = hn_ref[...]
        r2_ref[...] = r2_in[...]
        e2_ref[...] = e2_in[...]

    hn = hn_scr[...]

    def scores(p):
        return _dot_nt(u_ref[p * MXU_DIM:(p + 1) * MXU_DIM, :], hn)

    def gate_of(r, tok):
        gate = None
        for hd in range(PEER_HEADS):
            nb = jnp.broadcast_to(n1_ref[hd, r:r + 1, tok], (PEER_NKEYS, LANES)).astype(BF16)
            eb = jnp.broadcast_to(e1_ref[hd, r:r + 1, tok], (PEER_NKEYS, LANES)).astype(BF16)
            prod = e2_ref[hd, :, tok] * eb
            term = jnp.where(r2_ref[hd, :, tok] < nb, prod, jnp.zeros_like(prod))
            gate = term if gate is None else gate + term
        return gate

    def weights(p, a):
        cols = []
        for t0 in range(0, tt, LANES):
            tok = slice(t0, t0 + LANES)
            gate = jnp.concatenate([gate_of(2 * p, tok), gate_of(2 * p + 1, tok)], axis=0)
            cols.append(gate * _gelu(a[:, tok]).astype(BF16))
        return jnp.concatenate(cols, axis=1)

    a_next = scores(0)
    for p in range(steps):
        a_cur = a_next
        if p + 1 < steps:
            a_next = scores(p + 1)
        w = weights(p, a_cur)
        acc[...] += _dot(vt_ref[:, p * MXU_DIM:(p + 1) * MXU_DIM], w)

    @pl.when(j == pl.num_programs(1) - 1)
    def _():
        o_ref[...] = x_ref[...] + acc[...].T


def peer_dense(x, hn, u, vt, n1, e1, r2, e2, *, tt, ec):
    T, D = x.shape
    E = u.shape[0]
    H, NK = PEER_HEADS, PEER_NKEYS
    row_spec = pl.BlockSpec((H, ec // NK, tt), lambda i, j: (0, j, i))
    key_spec = pl.BlockSpec((H, NK, tt), lambda i, j: (0, 0, i))
    return pl.pallas_call(
        _peer_dense_kernel,
        grid=(T // tt, E // ec),
        in_specs=[
            pl.BlockSpec((tt, D), lambda i, j: (i, 0)),
            pl.BlockSpec((tt, D), lambda i, j: (i, 0)),
            pl.BlockSpec((ec, D), lambda i, j: (j, 0)),
            pl.BlockSpec((D, ec), lambda i, j: (0, j)),
            row_spec, row_spec, key_spec, key_spec,
        ],
        out_specs=pl.BlockSpec((tt, D), lambda i, j: (i, 0)),
        out_shape=jax.ShapeDtypeStruct((T, D), F32),
        scratch_shapes=[pltpu.VMEM((D, tt), F32), pltpu.VMEM((tt, D), BF16),
                        pltpu.VMEM((H, NK, tt), BF16), pltpu.VMEM((H, NK, tt), BF16)],
        compiler_params=_params("parallel", "arbitrary"),
        name="peer_dense",
    )(x, hn, u, vt, n1, e1, r2, e2)


def _final_norm_kernel(x_ref, g_ref, o_ref):
    o_ref[...] = _rmsnorm_rows(x_ref[...], g_ref[...])


def final_norm(x, g, *, tm):
    T, D = x.shape
    return pl.pallas_call(
        _final_norm_kernel,
        grid=(T // tm,),
        in_specs=[pl.BlockSpec((tm, D), lambda i: (i, 0)), pl.BlockSpec((1, D), lambda i: (0, 0))],
        out_specs=pl.BlockSpec((tm, D), lambda i: (i, 0)),
        out_shape=jax.ShapeDtypeStruct((T, D), F32),
        compiler_params=_params("parallel"),
        name="final_norm",
    )(x, g)


def _rope_tables(S):
    d = RET_HEAD_DIM
    inv = ROPE_BASE ** (-jnp.arange(0, d, 2, dtype=F32) / d)
    ang = jnp.arange(S, dtype=F32)[:, None] * inv[None, :]
    cos, sin = jnp.cos(ang), jnp.sin(ang)
    return jnp.concatenate([cos, cos], axis=1), jnp.concatenate([-sin, sin], axis=1)


def _tile(n, want):
    t = min(n, want)
    assert n % t == 0, (n, t)
    return t


def kernel(x, mem, mix_norm, w_in, sb_norm, ret_norm, w_out, xa_norm, mem_norm, xa_wq, xa_wkv, xa_wo,
           peer_norm, peer_wq, peer_keys, peer_u, peer_v, final_norm_g):
    B, S, D = x.shape
    M = mem.shape[1]
    T = B * S
    depth = w_in.shape[0]
    assert D == D_MODEL and S % SB_BLOCK == 0 and w_in.shape[2] == IN_COLS

    cos, sin = _rope_tables(S)
    log_gamma = jnp.log1p(-jnp.power(2.0, -5.0 - jnp.arange(RET_HEADS, dtype=F32)))
    log_gamma = jnp.broadcast_to(log_gamma[:, None, None], (RET_HEADS, 1, LANES))
    row = lambda v: v.reshape(1, -1).astype(F32)
    tm = _tile(T, 512)

    xt = x.reshape(T, D)
    mem2 = mem.reshape(B * M, D)
    for l in range(depth):
        proj = norm_matmul(xt, row(mix_norm[l]), w_in[l].astype(BF16), tm=tm, tn=IN_COLS // 7)
        proj = proj.reshape(B, S, IN_COLS)
        sb_o = sb_attention(proj, row(sb_norm[l]))
        ret_o = retention(proj, row(ret_norm[l]), cos, sin, log_gamma)
        w_o = w_out[l].astype(BF16)
        xt = out_proj(xt, sb_o.reshape(T, SB_WIDTH), ret_o.reshape(T, RET_WIDTH),
                      w_o[:SB_WIDTH], w_o[SB_WIDTH:], tm=tm)

        kv = norm_matmul(mem2, row(mem_norm[l]), xa_wkv[l].astype(BF16), tm=_tile(B * M, 512), tn=1024)
        xt = cross_attn(xt.reshape(B, S, D), row(xa_norm[l]), xa_wq[l].astype(BF16),
                        kv.reshape(B, M, 2 * D), xa_wo[l].astype(BF16), tq=_tile(S, 512)).reshape(T, D)

        tt = _tile(T, 256)
        hn, n1, e1, r2, e2 = peer_select(xt, row(peer_norm[l]), peer_wq[l].astype(BF16),
                                         peer_keys[l].astype(BF16), tt=tt)
        xt = peer_dense(xt, hn, peer_u[l].astype(BF16), peer_v[l].T.astype(BF16),
                        n1, e1, r2, e2, tt=tt, ec=2048)
    return final_norm(xt, row(final_norm_g), tm=tm).reshape(B, S, D)
```

```python
import functools
import math

import jax
import jax.numpy as jnp
from jax import lax
from jax.experimental import pallas as pl
from jax.experimental.pallas import tpu as pltpu

F32 = jnp.float32
BF16 = jnp.bfloat16
U32 = jnp.uint32

D_MODEL = 1024
SB_HEADS = 8
SB_HEAD_DIM = 64
SB_WIDTH = 512
SB_BLOCK = 128
RET_HEADS = 4
RET_HEAD_DIM = 128
RET_WIDTH = 512
RET_CHUNK = 128
ROPE_BASE = 10000.0
IN_COLS = 3 * SB_WIDTH + 4 * RET_WIDTH
XA_HEADS = 4
XA_HEAD_DIM = 256
PEER_HEADS = 8
PEER_NKEYS = 128
PEER_DQ = 256
PEER_TOPK = 16
EPS = 1e-6

LANES = 128
SUBLANES = 8
VMEM_LIMIT = 56 * 1024 * 1024
NEG_INF = float("-inf")
EXP_ZERO_BELOW = -104.0


def _params(*sem):
    return pltpu.CompilerParams(dimension_semantics=sem, vmem_limit_bytes=VMEM_LIMIT)


def _dot(a, b):
    return jnp.dot(a, b, preferred_element_type=F32)


def _dot_nt(a, b):
    return lax.dot_general(a, b, (((1,), (1,)), ((), ())), preferred_element_type=F32)


def _rmsnorm_rows(x, g):
    ms = jnp.mean(x * x, axis=-1, keepdims=True)
    return x * lax.rsqrt(ms + EPS) * g


def _unpack(words):
    return pltpu.bitcast(words, BF16)


def _pack(rows):
    return pltpu.bitcast(rows, U32)


def _norm_matmul_kernel(x_ref, g_ref, w_ref, o_ref, *, tn):
    h = _rmsnorm_rows(x_ref[...], g_ref[...]).astype(BF16)
    for n0 in range(0, w_ref.shape[1], tn):
        o_ref[:, n0:n0 + tn] = _dot(h, w_ref[:, n0:n0 + tn]).astype(o_ref.dtype)


def norm_matmul(x, g, w, *, tm, tn, out_dtype=BF16):
    T, D = x.shape
    N = w.shape[1]
    return pl.pallas_call(
        functools.partial(_norm_matmul_kernel, tn=tn),
        grid=(T // tm,),
        in_specs=[
            pl.BlockSpec((tm, D), lambda i: (i, 0)),
            pl.BlockSpec((1, D), lambda i: (0, 0)),
            pl.BlockSpec((D, N), lambda i: (0, 0)),
        ],
        out_specs=pl.BlockSpec((tm, N), lambda i: (i, 0)),
        out_shape=jax.ShapeDtypeStruct((T, N), out_dtype),
        compiler_params=_params("parallel"),
        name="norm_matmul",
    )(x, g, w)


def _softplus(z):
    return jnp.maximum(z, 0.0) + jnp.log1p(jnp.exp(-jnp.abs(z)))


def _sb_kernel(q_ref, k_ref, v_ref, g_ref, o_ref, acc, run):
    blk = SB_BLOCK
    pairs = SB_WIDTH // LANES
    i = pl.program_id(1)
    scale = SB_HEAD_DIM ** -0.5
    first = lax.broadcasted_iota(jnp.int32, (1, LANES), 1) < SB_HEAD_DIM
    row = lax.broadcasted_iota(jnp.int32, (2 * blk, blk), 0) % blk
    col = lax.broadcasted_iota(jnp.int32, (2 * blk, blk), 1)
    causal = col < row
    r2 = lax.broadcasted_iota(jnp.int32, (2 * blk, 2 * blk), 0) % blk
    c2 = lax.broadcasted_iota(jnp.int32, (2 * blk, 2 * blk), 1)
    suffix = jnp.where((c2 >= blk) | (r2 > c2), 1.0, 0.0).astype(BF16)

    def split_heads(t):
        zero = jnp.zeros_like(t)
        return jnp.concatenate([jnp.where(first, t, zero), jnp.where(first, zero, t)], axis=0)

    def add_key_block(j, diagonal):
        rows = pl.ds(pl.multiple_of(j * blk, blk), blk)
        col_of = [slice(p * LANES, (p + 1) * LANES) for p in range(pairs)]
        zs = [_dot_nt(split_heads(q_ref[:, c]), k_ref[rows, c]) * scale for c in col_of]
        sps = [_softplus(z) for z in zs]
        sums = []
        for sp in sps:
            lnb = jnp.where(causal, -sp, 0.0) if diagonal else -sp
            hi = lnb.astype(BF16)
            lo = (lnb - hi.astype(F32)).astype(BF16)
            sums.append(_dot(jnp.concatenate([hi, lo], axis=1), suffix))
        outs = []
        for p in range(pairs):
            stick, tot = sums[p][:, :blk], sums[p][:, blk:]
            if diagonal:
                a = jnp.where(causal, jnp.exp(zs[p] - sps[p] + stick), 0.0)
                run[p] = tot
            else:
                a = jnp.exp(zs[p] - sps[p] + stick + run[p])
                run[p] += tot
            a = a.astype(BF16)
            outs.append(_dot(jnp.concatenate([a[:blk], a[blk:]], axis=1),
                             split_heads(v_ref[rows, col_of[p]])))
        for p in range(pairs):
            acc[p] = outs[p] if diagonal else acc[p] + outs[p]

    def highest_run():
        m = run[0]
        for p in range(1, pairs):
            m = jnp.maximum(m, run[p])
        return jnp.max(m)

    add_key_block(i, True)

    def cond(carry):
        jj, top = carry
        return (jj <= i) & (top > EXP_ZERO_BELOW)

    def body(carry):
        jj, _ = carry
        add_key_block(i - jj, False)
        return jj + 1, highest_run()

    lax.while_loop(cond, body, (jnp.int32(1), highest_run()))

    for p in range(pairs):
        cols = slice(p * LANES, (p + 1) * LANES)
        o = acc[p]
        sq = o * o
        ms0 = jnp.sum(jnp.where(first, sq, 0.0), axis=-1, keepdims=True)
        ms1 = jnp.sum(jnp.where(first, 0.0, sq), axis=-1, keepdims=True)
        ms = jnp.where(first, ms0, ms1) * (1.0 / SB_HEAD_DIM)
        o_ref[:, cols] = (o * lax.rsqrt(ms + EPS) * g_ref[:, cols]).astype(o_ref.dtype)


def sb_attention(proj, sb_norm):
    B, S, _ = proj.shape
    pairs = SB_WIDTH // LANES
    return pl.pallas_call(
        _sb_kernel,
        grid=(B, S // SB_BLOCK),
        in_specs=[
            pl.BlockSpec((None, SB_BLOCK, SB_WIDTH), lambda b, i: (b, i, 0)),
            pl.BlockSpec((None, S, SB_WIDTH), lambda b, i: (b, 0, 1)),
            pl.BlockSpec((None, S, SB_WIDTH), lambda b, i: (b, 0, 2)),
            pl.BlockSpec((1, SB_WIDTH), lambda b, i: (0, 0)),
        ],
        out_specs=pl.BlockSpec((None, SB_BLOCK, SB_WIDTH), lambda b, i: (b, i, 0)),
        out_shape=jax.ShapeDtypeStruct((B, S, SB_WIDTH), BF16),
        scratch_shapes=[pltpu.VMEM((pairs, SB_BLOCK, LANES), F32),
                        pltpu.VMEM((pairs, 2 * SB_BLOCK, LANES), F32)],
        compiler_params=_params("parallel", "arbitrary"),
        name="sb_attention",
    )(proj, proj, proj, sb_norm)


def _retention_kernel(q_ref, k_ref, v_ref, gate_ref, cos_ref, sin_ref, lg_ref, g_ref, o_ref, state):
    C = RET_CHUNK
    n = pl.program_id(1)

    @pl.when(n == 0)
    def _():
        state[...] = jnp.zeros_like(state)

    cos, sin = cos_ref[...], sin_ref[...]

    def rope(t):
        t = t.astype(F32)
        return t * cos + pltpu.roll(t, RET_HEAD_DIM // 2, 1) * sin

    row = lax.broadcasted_iota(jnp.int32, (C, C), 0).astype(F32)
    col = lax.broadcasted_iota(jnp.int32, (C, C), 1).astype(F32)
    diff = row - col
    for hd in range(RET_HEADS):
        cols = slice(hd * RET_HEAD_DIM, (hd + 1) * RET_HEAD_DIM)
        lg = lg_ref[hd]
        decay = jnp.where(diff >= 0, jnp.exp(diff * lg), 0.0)
        zeta = jnp.exp((C - 1 - row) * lg)
        xi = jnp.exp((row + 1) * lg)
        decay_c = jnp.exp(C * lg)

        qc = rope(q_ref[:, cols])
        kc = rope(k_ref[:, cols]) * RET_HEAD_DIM ** -0.5
        vc = v_ref[:, cols]
        st = state[hd]
        inner_s = _dot_nt(qc.astype(BF16), kc.astype(BF16)) * decay
        o = _dot(inner_s.astype(BF16), vc) + _dot((qc * xi).astype(BF16), st.astype(BF16))
        state[hd] = decay_c * st + _dot((kc * zeta).T.astype(BF16), vc)

        mu = jnp.mean(o, axis=-1, keepdims=True)
        var = jnp.mean((o - mu) ** 2, axis=-1, keepdims=True)
        y = (o - mu) * lax.rsqrt(var + EPS) * g_ref[:, cols]
        gate = gate_ref[:, cols].astype(F32)
        o_ref[:, cols] = (y * (gate * jax.nn.sigmoid(gate))).astype(o_ref.dtype)


def retention(proj, ret_norm, cos, sin, log_gamma):
    B, S, _ = proj.shape
    C = RET_CHUNK
    base = 3 * SB_WIDTH // RET_WIDTH

    def col(group):
        return pl.BlockSpec((None, C, RET_WIDTH), lambda b, n: (b, n, base + group))

    return pl.pallas_call(
        _retention_kernel,
        grid=(B, S // C),
        in_specs=[
            col(0), col(1), col(2), col(3),
            pl.BlockSpec((C, LANES), lambda b, n: (n, 0)),
            pl.BlockSpec((C, LANES), lambda b, n: (n, 0)),
            pl.BlockSpec((RET_HEADS, 1, LANES), lambda b, n: (0, 0, 0)),
            pl.BlockSpec((1, RET_WIDTH), lambda b, n: (0, 0)),
        ],
        out_specs=pl.BlockSpec((None, C, RET_WIDTH), lambda b, n: (b, n, 0)),
        out_shape=jax.ShapeDtypeStruct((B, S, RET_WIDTH), BF16),
        scratch_shapes=[pltpu.VMEM((RET_HEADS, RET_HEAD_DIM, RET_HEAD_DIM), F32)],
        compiler_params=_params("parallel", "arbitrary"),
        name="retention",
    )(proj, proj, proj, proj, cos, sin, log_gamma, ret_norm)


def _out_proj_kernel(x_ref, a_ref, b_ref, wa_ref, wb_ref, o_ref):
    o_ref[...] = x_ref[...] + _dot(a_ref[...], wa_ref[...]) + _dot(b_ref[...], wb_ref[...])


def out_proj(x, a, b, wa, wb, *, tm):
    T, D = x.shape
    row = lambda i: (i, 0)
    const = lambda i: (0, 0)
    return pl.pallas_call(
        _out_proj_kernel,
        grid=(T // tm,),
        in_specs=[
            pl.BlockSpec((tm, D), row),
            pl.BlockSpec((tm, a.shape[1]), row),
            pl.BlockSpec((tm, b.shape[1]), row),
            pl.BlockSpec(wa.shape, const),
            pl.BlockSpec(wb.shape, const),
        ],
        out_specs=pl.BlockSpec((tm, D), row),
        out_shape=jax.ShapeDtypeStruct((T, D), F32),
        compiler_params=_params("parallel"),
        name="out_proj",
    )(x, a, b, wa, wb)


def _cross_attn_kernel(x_ref, g_ref, wq_ref, kv_ref, wo_ref, o_ref):
    x = x_ref[...]
    h = _rmsnorm_rows(x, g_ref[...]).astype(BF16)
    q = _dot(h, wq_ref[...]).astype(BF16)
    outs = []
    for hd in range(XA_HEADS):
        lo = hd * XA_HEAD_DIM
        k = kv_ref[:, lo:lo + XA_HEAD_DIM]
        v = kv_ref[:, D_MODEL + lo:D_MODEL + lo + XA_HEAD_DIM]
        s = _dot_nt(q[:, lo:lo + XA_HEAD_DIM], k) * XA_HEAD_DIM ** -0.5
        e = jnp.exp(s - jnp.max(s, axis=-1, keepdims=True))
        p = e / jnp.sum(e, axis=-1, keepdims=True)
        outs.append(_dot(p.astype(BF16), v).astype(BF16))
    o_ref[...] = x + _dot(jnp.concatenate(outs, axis=1), wo_ref[...])


def cross_attn(x, g, wq, kv, wo, *, tq):
    B, S, D = x.shape
    M = kv.shape[1]
    const = lambda b, i: (0, 0)
    return pl.pallas_call(
        _cross_attn_kernel,
        grid=(B, S // tq),
        in_specs=[
            pl.BlockSpec((None, tq, D), lambda b, i: (b, i, 0)),
            pl.BlockSpec((1, D), const),
            pl.BlockSpec((D, D), const),
            pl.BlockSpec((None, M, 2 * D), lambda b, i: (b, 0, 0)),
            pl.BlockSpec((D, D), const),
        ],
        out_specs=pl.BlockSpec((None, tq, D), lambda b, i: (b, i, 0)),
        out_shape=jax.ShapeDtypeStruct((B, S, D), F32),
        compiler_params=_params("parallel", "arbitrary"),
        name="cross_attn",
    )(x, g, wq, kv, wo)


def _kth_value(s, k):
    for _ in range(k - 1):
        s = jnp.where(s == jnp.max(s, axis=0, keepdims=True), NEG_INF, s)
    return jnp.max(s, axis=0, keepdims=True)


INT32_MIN = -2 ** 31
INT32_MAX = 2 ** 31 - 1


def _order_key(v):
    b = pltpu.bitcast(v, jnp.int32) if v.dtype == F32 else v
    k = b ^ ((b >> 31) & jnp.int32(INT32_MAX))
    return k if v.dtype == F32 else pltpu.bitcast(k, F32)


def _top_values(s, k):
    rowid = lax.broadcasted_iota(jnp.int32, (k, s.shape[1]), 0)
    key = _order_key(s)
    vals, stacked = [], jnp.full((k, s.shape[1]), INT32_MIN, jnp.int32)
    for r in range(k):
        m = jnp.max(key, axis=0, keepdims=True)
        vals.append(_order_key(m))
        stacked = jnp.where(rowid == r, m, stacked)
        key = jnp.where(key == m, jnp.int32(INT32_MIN + r), key)
    rank = jnp.where(key < jnp.int32(INT32_MIN + k), key & jnp.int32(INT32_MAX), k).astype(F32)
    return vals, _order_key(stacked), rank


def _pair_candidates(t1, t1s, t2, t2s):
    K, sub = PEER_TOPK, 8
    rowid = lax.broadcasted_iota(jnp.int32, (sub, t1s.shape[1]), 0)
    pieces = [t1[0] + t2s]
    for a in range(1, sub):
        pieces.append(jnp.where(rowid < K // (a + 1), t1[a] + t2s[:sub], NEG_INF))
    pieces.append(t1s[sub:] + t2[0])
    return jnp.concatenate(pieces, axis=0)


def _peer_select_kernel(x_ref, g_ref, wq_ref, keys_ref, hn_ref, n1_ref, e1_ref, r2_ref, e2_ref):
    K = PEER_TOPK
    h = _rmsnorm_rows(x_ref[...], g_ref[...]).astype(BF16)
    hn_ref[...] = _pack(h)
    q = _dot(h, wq_ref[...]).astype(BF16)
    half = PEER_DQ // 2
    for hd in range(PEER_HEADS):
        s1 = _dot_nt(keys_ref[hd, 0], q[:, hd * PEER_DQ:hd * PEER_DQ + half])
        s2 = _dot_nt(keys_ref[hd, 1], q[:, hd * PEER_DQ + half:(hd + 1) * PEER_DQ])
        t1, t1s, rank1 = _top_values(s1, K)
        t2, t2s, rank2 = _top_values(s2, K)
        cand = _pair_candidates(t1, t1s, t2, t2s)
        tau = _kth_value(cand, K)
        z = jnp.sum(jnp.where(cand >= tau, jnp.exp(cand - (t1[0] + t2[0])), 0.0), axis=0, keepdims=True)
        rank1_b = rank1.astype(BF16)
        n1 = jnp.zeros_like(rank1_b)
        for a in range(K):
            n_a = jnp.sum(jnp.where(t1[a] + t2s >= tau, 1.0, 0.0), axis=0, keepdims=True)
            n1 = jnp.where(rank1_b == float(a), jnp.broadcast_to(n_a, s1.shape).astype(BF16), n1)
        n1_ref[hd] = n1.astype(F32)
        e1_ref[hd] = jnp.where(rank1 < float(K), jnp.exp(s1 - t1[0]), 0.0)
        r2_ref[hd] = _pack(rank2.astype(BF16))
        e2_ref[hd] = _pack((jnp.exp(s2 - t2[0]) / z).astype(BF16))


def peer_select(x, g, wq, keys, *, tt):
    T, D = x.shape
    H, NK = PEER_HEADS, PEER_NKEYS
    const2 = lambda i: (0, 0)
    row_spec = pl.BlockSpec((H, NK, tt), lambda i: (0, 0, i))
    key_spec = pl.BlockSpec((H, NK // 2, tt), lambda i: (0, 0, i))
    return pl.pallas_call(
        _peer_select_kernel,
        grid=(T // tt,),
        in_specs=[
            pl.BlockSpec((tt, D), lambda i: (i, 0)),
            pl.BlockSpec((1, D), const2),
            pl.BlockSpec(wq.shape, const2),
            pl.BlockSpec(keys.shape, lambda i: (0, 0, 0, 0)),
        ],
        out_specs=[pl.BlockSpec((tt // 2, D), lambda i: (i, 0)), row_spec, row_spec, key_spec, key_spec],
        out_shape=[
            jax.ShapeDtypeStruct((T // 2, D), U32),
            jax.ShapeDtypeStruct((H, NK, T), F32),
            jax.ShapeDtypeStruct((H, NK, T), F32),
            jax.ShapeDtypeStruct((H, NK // 2, T), U32),
            jax.ShapeDtypeStruct((H, NK // 2, T), U32),
        ],
        compiler_params=_params("parallel"),
        name="peer_select",
    )(x, g, wq, keys)


def _gelu(a):
    return a * 0.5 * (1.0 + lax.erf(a * (1.0 / math.sqrt(2.0))))


MXU_DIM = 256
OUT_PAIRS = 2


def _peer_dense_kernel(x_ref, hn_ref, u_ref, vt_ref, n1_ref, e1_ref, r2_ref, e2_ref, g_ref, o_ref, acc,
                       *, final):
    j = pl.program_id(1)
    tt = x_ref.shape[0]
    pairs = 2 * u_ref.shape[0] // MXU_DIM

    @pl.when(j == 0)
    def _():
        acc[...] = jnp.zeros_like(acc)

    hn = _unpack(hn_ref[...])

    def scores(p):
        rows = slice(p * MXU_DIM // 2, (p + 1) * MXU_DIM // 2)
        return _dot_nt(_unpack(u_ref[rows, :]), hn)

    def gate_of(r, tok):
        gate = None
        for hd in range(PEER_HEADS):
            nb = jnp.broadcast_to(n1_ref[hd, r:r + 1, tok], (PEER_NKEYS, LANES)).astype(BF16)
            eb = jnp.broadcast_to(e1_ref[hd, r:r + 1, tok], (PEER_NKEYS, LANES)).astype(BF16)
            prod = _unpack(e2_ref[hd, :, tok]) * eb
            term = jnp.where(_unpack(r2_ref[hd, :, tok]) < nb, prod, jnp.zeros_like(prod))
            gate = term if gate is None else gate + term
        return gate

    def weights(p, a):
        cols = []
        for t0 in range(0, tt, LANES):
            tok = slice(t0, t0 + LANES)
            gate = jnp.concatenate([gate_of(2 * p, tok), gate_of(2 * p + 1, tok)], axis=0)
            cols.append(gate * _gelu(a[:, tok]).astype(BF16))
        return jnp.concatenate(cols, axis=1)

    a_next = scores(0)
    ws = []
    for p in range(pairs):
        a_cur = a_next
        if p + 1 < pairs:
            a_next = scores(p + 1)
        ws.append(weights(p, a_cur))
        if len(ws) == OUT_PAIRS:
            cols = slice((p + 1 - OUT_PAIRS) * MXU_DIM, (p + 1) * MXU_DIM)
            acc[...] += _dot(_unpack(vt_ref[:, cols]), jnp.concatenate(ws, axis=0))
            ws = []

    @pl.when(j == pl.num_programs(1) - 1)
    def _():
        o = x_ref[...] + acc[...].T
        o_ref[...] = _rmsnorm_rows(o, g_ref[...]) if final else o


def peer_dense(x, hn, u, vt, n1, e1, r2, e2, g, *, tt, ec, final):
    T, D = x.shape
    E = 2 * u.shape[0]
    H, NK = PEER_HEADS, PEER_NKEYS
    row_spec = pl.BlockSpec((H, ec // NK, tt), lambda i, j: (0, j, i))
    key_spec = pl.BlockSpec((H, NK // 2, tt), lambda i, j: (0, 0, i))
    return pl.pallas_call(
        functools.partial(_peer_dense_kernel, final=final),
        grid=(T // tt, E // ec),
        in_specs=[
            pl.BlockSpec((tt, D), lambda i, j: (i, 0)),
            pl.BlockSpec((tt // 2, D), lambda i, j: (i, 0)),
            pl.BlockSpec((ec // 2, D), lambda i, j: (j, 0)),
            pl.BlockSpec((D // 2, ec), lambda i, j: (0, j)),
            row_spec, row_spec, key_spec, key_spec,
            pl.BlockSpec((1, D), lambda i, j: (0, 0)),
        ],
        out_specs=pl.BlockSpec((tt, D), lambda i, j: (i, 0)),
        out_shape=jax.ShapeDtypeStruct((T, D), F32),
        scratch_shapes=[pltpu.VMEM((D, tt), F32)],
        compiler_params=_params("parallel", "arbitrary"),
        name="peer_dense",
    )(x, hn, u, vt, n1, e1, r2, e2, g)


def _pack_rows_kernel(w_ref, o_ref):
    o_ref[...] = _pack(w_ref[...].astype(BF16))


def pack_rows(w, *, tr):
    R, C = w.shape
    return pl.pallas_call(
        _pack_rows_kernel,
        grid=(R // tr,),
        in_specs=[pl.BlockSpec((tr, C), lambda i: (i, 0))],
        out_specs=pl.BlockSpec((tr // 2, C), lambda i: (i, 0)),
        out_shape=jax.ShapeDtypeStruct((R // 2, C), U32),
        compiler_params=_params("parallel"),
        name="pack_rows",
    )(w)


def _pack_transposed_kernel(w_ref, o_ref):
    o_ref[...] = _pack(w_ref[...].T.astype(BF16))


def pack_transposed(w, *, tr):
    R, C = w.shape
    return pl.pallas_call(
        _pack_transposed_kernel,
        grid=(R // tr,),
        in_specs=[pl.BlockSpec((tr, C), lambda i: (i, 0))],
        out_specs=pl.BlockSpec((C // 2, tr), lambda i: (0, i)),
        out_shape=jax.ShapeDtypeStruct((C // 2, R), U32),
        compiler_params=_params("parallel"),
        name="pack_transposed",
    )(w)


def _rope_tables(S):
    d = RET_HEAD_DIM
    inv = ROPE_BASE ** (-jnp.arange(0, d, 2, dtype=F32) / d)
    ang = jnp.arange(S, dtype=F32)[:, None] * inv[None, :]
    cos, sin = jnp.cos(ang), jnp.sin(ang)
    return jnp.concatenate([cos, cos], axis=1), jnp.concatenate([-sin, sin], axis=1)


def _tile(n, want):
    t = min(n, want)
    assert n % t == 0, (n, t)
    return t


def kernel(x, mem, mix_norm, w_in, sb_norm, ret_norm, w_out, xa_norm, mem_norm, xa_wq, xa_wkv, xa_wo,
           peer_norm, peer_wq, peer_keys, peer_u, peer_v, final_norm_g):
    B, S, D = x.shape
    M = mem.shape[1]
    T = B * S
    depth = w_in.shape[0]
    assert D == D_MODEL and S % SB_BLOCK == 0 and w_in.shape[2] == IN_COLS

    cos, sin = _rope_tables(S)
    log_gamma = jnp.log1p(-jnp.power(2.0, -5.0 - jnp.arange(RET_HEADS, dtype=F32)))
    log_gamma = jnp.broadcast_to(log_gamma[:, None, None], (RET_HEADS, 1, LANES))
    row = lambda v: v.reshape(1, -1).astype(F32)
    tm = _tile(T, 512)

    xt = x.reshape(T, D)
    mem2 = mem.reshape(B * M, D)
    for l in range(depth):
        proj = norm_matmul(xt, row(mix_norm[l]), w_in[l].astype(BF16), tm=tm, tn=IN_COLS // 7)
        proj = proj.reshape(B, S, IN_COLS)
        sb_o = sb_attention(proj, row(sb_norm[l]))
        ret_o = retention(proj, row(ret_norm[l]), cos, sin, log_gamma)
        w_o = w_out[l].astype(BF16)
        xt = out_proj(xt, sb_o.reshape(T, SB_WIDTH), ret_o.reshape(T, RET_WIDTH),
                      w_o[:SB_WIDTH], w_o[SB_WIDTH:], tm=tm)

        kv = norm_matmul(mem2, row(mem_norm[l]), xa_wkv[l].astype(BF16), tm=_tile(B * M, 512), tn=1024)
        xt = cross_attn(xt.reshape(B, S, D), row(xa_norm[l]), xa_wq[l].astype(BF16),
                        kv.reshape(B, M, 2 * D), xa_wo[l].astype(BF16), tq=_tile(S, 512)).reshape(T, D)

        hn, n1, e1, r2, e2 = peer_select(xt, row(peer_norm[l]), peer_wq[l].astype(BF16),
                                         peer_keys[l].astype(BF16), tt=_tile(T, 256))
        xt = peer_dense(xt, hn, pack_rows(peer_u[l], tr=1024), pack_transposed(peer_v[l], tr=512),
                        n1, e1, r2, e2, row(final_norm_g), tt=_tile(T, 512), ec=1024,
                        final=l == depth - 1)
    return xt.reshape(B, S, D)
```

```python
import functools
import math

import jax
import jax.numpy as jnp
from jax import lax
from jax.experimental import pallas as pl
from jax.experimental.pallas import tpu as pltpu

F32 = jnp.float32
BF16 = jnp.bfloat16
U32 = jnp.uint32

D_MODEL = 1024
SB_HEADS = 8
SB_HEAD_DIM = 64
SB_WIDTH = 512
SB_BLOCK = 128
RET_HEADS = 4
RET_HEAD_DIM = 128
RET_WIDTH = 512
RET_CHUNK = 128
ROPE_BASE = 10000.0
IN_COLS = 3 * SB_WIDTH + 4 * RET_WIDTH
XA_HEADS = 4
XA_HEAD_DIM = 256
PEER_HEADS = 8
PEER_NKEYS = 128
PEER_DQ = 256
PEER_TOPK = 16
EPS = 1e-6

LANES = 128
SUBLANES = 8
VMEM_LIMIT = 56 * 1024 * 1024
NEG_INF = float("-inf")
EXP_ZERO_BELOW = -88.0


def _params(*sem):
    return pltpu.CompilerParams(dimension_semantics=sem, vmem_limit_bytes=VMEM_LIMIT)


def _dot(a, b):
    return jnp.dot(a, b, preferred_element_type=F32)


def _dot_nt(a, b):
    return lax.dot_general(a, b, (((1,), (1,)), ((), ())), preferred_element_type=F32)


def _rmsnorm_rows(x, g):
    ms = jnp.mean(x * x, axis=-1, keepdims=True)
    return x * lax.rsqrt(ms + EPS) * g


def _unpack(words):
    return pltpu.bitcast(words, BF16)


def _pack(rows):
    return pltpu.bitcast(rows, U32)


def _norm_matmul_kernel(x_ref, g_ref, w_ref, o_ref, *, tn):
    h = _rmsnorm_rows(x_ref[...], g_ref[...]).astype(BF16)
    for n0 in range(0, w_ref.shape[1], tn):
        o_ref[:, n0:n0 + tn] = _dot(h, w_ref[:, n0:n0 + tn]).astype(o_ref.dtype)


def norm_matmul(x, g, w, *, tm, tn, out_dtype=BF16):
    T, D = x.shape
    N = w.shape[1]
    return pl.pallas_call(
        functools.partial(_norm_matmul_kernel, tn=tn),
        grid=(T // tm,),
        in_specs=[
            pl.BlockSpec((tm, D), lambda i: (i, 0)),
            pl.BlockSpec((1, D), lambda i: (0, 0)),
            pl.BlockSpec((D, N), lambda i: (0, 0)),
        ],
        out_specs=pl.BlockSpec((tm, N), lambda i: (i, 0)),
        out_shape=jax.ShapeDtypeStruct((T, N), out_dtype),
        compiler_params=_params("parallel"),
        name="norm_matmul",
    )(x, g, w)


def _softplus(z):
    return jnp.maximum(z, 0.0) + jnp.log(1.0 + jnp.exp(-jnp.abs(z)))


def _sb_kernel(q_ref, k_ref, v_ref, g_ref, o_ref, acc, run):
    blk = SB_BLOCK
    pairs = SB_WIDTH // LANES
    i = pl.program_id(1)
    scale = SB_HEAD_DIM ** -0.5
    first = lax.broadcasted_iota(jnp.int32, (1, LANES), 1) < SB_HEAD_DIM
    row = lax.broadcasted_iota(jnp.int32, (2 * blk, blk), 0) % blk
    col = lax.broadcasted_iota(jnp.int32, (2 * blk, blk), 1)
    causal = col < row
    r2 = lax.broadcasted_iota(jnp.int32, (2 * blk, 2 * blk), 0) % blk
    c2 = lax.broadcasted_iota(jnp.int32, (2 * blk, 2 * blk), 1)
    suffix = jnp.where((c2 >= blk) | (r2 > c2), 1.0, 0.0).astype(BF16)

    def split_heads(t):
        zero = jnp.zeros_like(t)
        return jnp.concatenate([jnp.where(first, t, zero), jnp.where(first, zero, t)], axis=0)

    def add_key_block(j, diagonal):
        rows = pl.ds(pl.multiple_of(j * blk, blk), blk)
        col_of = [slice(p * LANES, (p + 1) * LANES) for p in range(pairs)]
        zs = [_dot_nt(split_heads(q_ref[:, c]), k_ref[rows, c]) * scale for c in col_of]
        sps = [_softplus(z) for z in zs]
        sums = []
        for sp in sps:
            lnb = jnp.where(causal, -sp, 0.0) if diagonal else -sp
            hi = lnb.astype(BF16)
            lo = (lnb - hi.astype(F32)).astype(BF16)
            sums.append(_dot(jnp.concatenate([hi, lo], axis=1), suffix))
        outs = []
        for p in range(pairs):
            stick, tot = sums[p][:, :blk], sums[p][:, blk:]
            if diagonal:
                a = jnp.where(causal, jnp.exp(zs[p] - sps[p] + stick), 0.0)
                run[p] = tot
            else:
                a = jnp.exp(zs[p] - sps[p] + stick + run[p])
                run[p] += tot
            a = a.astype(BF16)
            outs.append(_dot(jnp.concatenate([a[:blk], a[blk:]], axis=1),
                             split_heads(v_ref[rows, col_of[p]])))
        for p in range(pairs):
            acc[p] = outs[p] if diagonal else acc[p] + outs[p]

    def highest_run():
        m = run[0]
        for p in range(1, pairs):
            m = jnp.maximum(m, run[p])
        return jnp.max(m)

    add_key_block(i, True)

    def cond(carry):
        jj, top = carry
        return (jj <= i) & (top > EXP_ZERO_BELOW)

    def body(carry):
        jj, _ = carry
        add_key_block(i - jj, False)
        return jj + 1, highest_run()

    lax.while_loop(cond, body, (jnp.int32(1), highest_run()))

    for p in range(pairs):
        cols = slice(p * LANES, (p + 1) * LANES)
        o = acc[p]
        sq = o * o
        ms0 = jnp.sum(jnp.where(first, sq, 0.0), axis=-1, keepdims=True)
        ms1 = jnp.sum(jnp.where(first, 0.0, sq), axis=-1, keepdims=True)
        ms = jnp.where(first, ms0, ms1) * (1.0 / SB_HEAD_DIM)
        o_ref[:, cols] = (o * lax.rsqrt(ms + EPS) * g_ref[:, cols]).astype(o_ref.dtype)


def sb_attention(proj, sb_norm):
    B, S, _ = proj.shape
    pairs = SB_WIDTH // LANES
    return pl.pallas_call(
        _sb_kernel,
        grid=(B, S // SB_BLOCK),
        in_specs=[
            pl.BlockSpec((None, SB_BLOCK, SB_WIDTH), lambda b, i: (b, i, 0)),
            pl.BlockSpec((None, S, SB_WIDTH), lambda b, i: (b, 0, 1)),
            pl.BlockSpec((None, S, SB_WIDTH), lambda b, i: (b, 0, 2)),
            pl.BlockSpec((1, SB_WIDTH), lambda b, i: (0, 0)),
        ],
        out_specs=pl.BlockSpec((None, SB_BLOCK, SB_WIDTH), lambda b, i: (b, i, 0)),
        out_shape=jax.ShapeDtypeStruct((B, S, SB_WIDTH), BF16),
        scratch_shapes=[pltpu.VMEM((pairs, SB_BLOCK, LANES), F32),
                        pltpu.VMEM((pairs, 2 * SB_BLOCK, LANES), F32)],
        compiler_params=_params("parallel", "arbitrary"),
        name="sb_attention",
    )(proj, proj, proj, sb_norm)


def _retention_kernel(q_ref, k_ref, v_ref, gate_ref, cos_ref, sin_ref, lg_ref, g_ref, o_ref, state):
    C = RET_CHUNK
    n = pl.program_id(1)

    @pl.when(n == 0)
    def _():
        state[...] = jnp.zeros_like(state)

    cos, sin = cos_ref[...], sin_ref[...]

    def rope(t):
        t = t.astype(F32)
        return t * cos + pltpu.roll(t, RET_HEAD_DIM // 2, 1) * sin

    row = lax.broadcasted_iota(jnp.int32, (C, C), 0).astype(F32)
    col = lax.broadcasted_iota(jnp.int32, (C, C), 1).astype(F32)
    diff = row - col
    heads = range(RET_HEADS)
    col_of = [slice(hd * RET_HEAD_DIM, (hd + 1) * RET_HEAD_DIM) for hd in heads]
    lgs = [lg_ref[hd] for hd in heads]
    qcs = [rope(q_ref[:, c]) for c in col_of]
    kcs = [rope(k_ref[:, c]) * RET_HEAD_DIM ** -0.5 for c in col_of]
    inner = [_dot_nt(qcs[hd].astype(BF16), kcs[hd].astype(BF16))
             * jnp.where(diff >= 0, jnp.exp(diff * lgs[hd]), 0.0) for hd in heads]
    sts = [state[hd] for hd in heads]
    outs = [_dot(inner[hd].astype(BF16), v_ref[:, col_of[hd]])
            + _dot((qcs[hd] * jnp.exp((row + 1) * lgs[hd])).astype(BF16), sts[hd].astype(BF16))
            for hd in heads]
    for hd in heads:
        zeta = jnp.exp((C - 1 - row) * lgs[hd])
        state[hd] = jnp.exp(C * lgs[hd]) * sts[hd] + _dot((kcs[hd] * zeta).T.astype(BF16),
                                                          v_ref[:, col_of[hd]])
    for hd in heads:
        o = outs[hd]
        mu = jnp.mean(o, axis=-1, keepdims=True)
        var = jnp.mean((o - mu) ** 2, axis=-1, keepdims=True)
        y = (o - mu) * lax.rsqrt(var + EPS) * g_ref[:, col_of[hd]]
        gate = gate_ref[:, col_of[hd]].astype(F32)
        o_ref[:, col_of[hd]] = (y * (gate * jax.nn.sigmoid(gate))).astype(o_ref.dtype)


def retention(proj, ret_norm, cos, sin, log_gamma):
    B, S, _ = proj.shape
    C = RET_CHUNK
    base = 3 * SB_WIDTH // RET_WIDTH

    def col(group):
        return pl.BlockSpec((None, C, RET_WIDTH), lambda b, n: (b, n, base + group))

    return pl.pallas_call(
        _retention_kernel,
        grid=(B, S // C),
        in_specs=[
            col(0), col(1), col(2), col(3),
            pl.BlockSpec((C, LANES), lambda b, n: (n, 0)),
            pl.BlockSpec((C, LANES), lambda b, n: (n, 0)),
            pl.BlockSpec((RET_HEADS, 1, LANES), lambda b, n: (0, 0, 0)),
            pl.BlockSpec((1, RET_WIDTH), lambda b, n: (0, 0)),
        ],
        out_specs=pl.BlockSpec((None, C, RET_WIDTH), lambda b, n: (b, n, 0)),
        out_shape=jax.ShapeDtypeStruct((B, S, RET_WIDTH), BF16),
        scratch_shapes=[pltpu.VMEM((RET_HEADS, RET_HEAD_DIM, RET_HEAD_DIM), F32)],
        compiler_params=_params("parallel", "arbitrary"),
        name="retention",
    )(proj, proj, proj, proj, cos, sin, log_gamma, ret_norm)


def _out_proj_kernel(x_ref, a_ref, b_ref, wa_ref, wb_ref, o_ref):
    o_ref[...] = x_ref[...] + _dot(a_ref[...], wa_ref[...]) + _dot(b_ref[...], wb_ref[...])


def out_proj(x, a, b, wa, wb, *, tm):
    T, D = x.shape
    row = lambda i: (i, 0)
    const = lambda i: (0, 0)
    return pl.pallas_call(
        _out_proj_kernel,
        grid=(T // tm,),
        in_specs=[
            pl.BlockSpec((tm, D), row),
            pl.BlockSpec((tm, a.shape[1]), row),
            pl.BlockSpec((tm, b.shape[1]), row),
            pl.BlockSpec(wa.shape, const),
            pl.BlockSpec(wb.shape, const),
        ],
        out_specs=pl.BlockSpec((tm, D), row),
        out_shape=jax.ShapeDtypeStruct((T, D), F32),
        compiler_params=_params("parallel"),
        name="out_proj",
    )(x, a, b, wa, wb)


def _cross_attn_kernel(x_ref, g_ref, wq_ref, kv_ref, wo_ref, o_ref):
    x = x_ref[...]
    h = _rmsnorm_rows(x, g_ref[...]).astype(BF16)
    q = _dot(h, wq_ref[...]).astype(BF16)
    outs = []
    for hd in range(XA_HEADS):
        lo = hd * XA_HEAD_DIM
        k = kv_ref[:, lo:lo + XA_HEAD_DIM]
        v = kv_ref[:, D_MODEL + lo:D_MODEL + lo + XA_HEAD_DIM]
        s = _dot_nt(q[:, lo:lo + XA_HEAD_DIM], k) * XA_HEAD_DIM ** -0.5
        e = jnp.exp(s - jnp.max(s, axis=-1, keepdims=True))
        p = e / jnp.sum(e, axis=-1, keepdims=True)
        outs.append(_dot(p.astype(BF16), v).astype(BF16))
    o_ref[...] = x + _dot(jnp.concatenate(outs, axis=1), wo_ref[...])


def cross_attn(x, g, wq, kv, wo, *, tq):
    B, S, D = x.shape
    M = kv.shape[1]
    const = lambda b, i: (0, 0)
    return pl.pallas_call(
        _cross_attn_kernel,
        grid=(B, S // tq),
        in_specs=[
            pl.BlockSpec((None, tq, D), lambda b, i: (b, i, 0)),
            pl.BlockSpec((1, D), const),
            pl.BlockSpec((D, D), const),
            pl.BlockSpec((None, M, 2 * D), lambda b, i: (b, 0, 0)),
            pl.BlockSpec((D, D), const),
        ],
        out_specs=pl.BlockSpec((None, tq, D), lambda b, i: (b, i, 0)),
        out_shape=jax.ShapeDtypeStruct((B, S, D), F32),
        compiler_params=_params("parallel", "arbitrary"),
        name="cross_attn",
    )(x, g, wq, kv, wo)


def _kth_value(s, k):
    for _ in range(k - 1):
        s = jnp.where(s == jnp.max(s, axis=0, keepdims=True), NEG_INF, s)
    return jnp.max(s, axis=0, keepdims=True)


INT32_MIN = -2 ** 31
INT32_MAX = 2 ** 31 - 1


def _order_key(v):
    b = pltpu.bitcast(v, jnp.int32) if v.dtype == F32 else v
    k = b ^ ((b >> 31) & jnp.int32(INT32_MAX))
    return k if v.dtype == F32 else pltpu.bitcast(k, F32)


def _top_values(s, k):
    rowid = lax.broadcasted_iota(jnp.int32, (k, s.shape[1]), 0)
    key = _order_key(s)
    vals, stacked = [], jnp.full((k, s.shape[1]), INT32_MIN, jnp.int32)
    for r in range(k):
        m = jnp.max(key, axis=0, keepdims=True)
        vals.append(_order_key(m))
        stacked = jnp.where(rowid == r, m, stacked)
        key = jnp.where(key == m, jnp.int32(INT32_MIN + r), key)
    rank = jnp.where(key < jnp.int32(INT32_MIN + k), key & jnp.int32(INT32_MAX), k).astype(F32)
    return vals, _order_key(stacked), rank


def _pair_candidates(t1, t1s, t2, t2s):
    K, sub = PEER_TOPK, 8
    rowid = lax.broadcasted_iota(jnp.int32, (sub, t1s.shape[1]), 0)
    pieces = [t1[0] + t2s]
    for a in range(1, sub):
        pieces.append(jnp.where(rowid < K // (a + 1), t1[a] + t2s[:sub], NEG_INF))
    pieces.append(t1s[sub:] + t2[0])
    return jnp.concatenate(pieces, axis=0)


def _peer_select_kernel(x_ref, g_ref, wq_ref, keys_ref, hnt_ref, n1_ref, e1_ref, r2_ref, e2_ref):
    K = PEER_TOPK
    hf = _rmsnorm_rows(x_ref[...], g_ref[...])
    hnt_ref[...] = _pack(hf.T.astype(BF16))
    q = _dot(hf.astype(BF16), wq_ref[...]).astype(BF16)
    half = PEER_DQ // 2
    for hd in range(PEER_HEADS):
        s1 = _dot_nt(keys_ref[hd, 0], q[:, hd * PEER_DQ:hd * PEER_DQ + half])
        s2 = _dot_nt(keys_ref[hd, 1], q[:, hd * PEER_DQ + half:(hd + 1) * PEER_DQ])
        t1, t1s, rank1 = _top_values(s1, K)
        t2, t2s, rank2 = _top_values(s2, K)
        cand = _pair_candidates(t1, t1s, t2, t2s)
        tau = _kth_value(cand, K)
        z = jnp.sum(jnp.where(cand >= tau, jnp.exp(cand - (t1[0] + t2[0])), 0.0), axis=0, keepdims=True)
        rank1_b = rank1.astype(BF16)
        n1 = jnp.zeros_like(rank1_b)
        for a in range(K):
            n_a = jnp.sum(jnp.where(t1[a] + t2s >= tau, 1.0, 0.0), axis=0, keepdims=True)
            n1 = jnp.where(rank1_b == float(a), jnp.broadcast_to(n_a, s1.shape).astype(BF16), n1)
        n1_ref[hd] = n1.astype(F32)
        e1_ref[hd] = jnp.where(rank1 < float(K), jnp.exp(s1 - t1[0]), 0.0)
        r2_ref[hd] = _pack(rank2.astype(BF16))
        e2_ref[hd] = _pack((jnp.exp(s2 - t2[0]) / z).astype(BF16))


def peer_select(x, g, wq, keys, *, tt):
    T, D = x.shape
    H, NK = PEER_HEADS, PEER_NKEYS
    const2 = lambda i: (0, 0)
    row_spec = pl.BlockSpec((H, NK, tt), lambda i: (0, 0, i))
    key_spec = pl.BlockSpec((H, NK // 2, tt), lambda i: (0, 0, i))
    return pl.pallas_call(
        _peer_select_kernel,
        grid=(T // tt,),
        in_specs=[
            pl.BlockSpec((tt, D), lambda i: (i, 0)),
            pl.BlockSpec((1, D), const2),
            pl.BlockSpec(wq.shape, const2),
            pl.BlockSpec(keys.shape, lambda i: (0, 0, 0, 0)),
        ],
        out_specs=[pl.BlockSpec((D // 2, tt), lambda i: (0, i)), row_spec, row_spec, key_spec, key_spec],
        out_shape=[
            jax.ShapeDtypeStruct((D // 2, T), U32),
            jax.ShapeDtypeStruct((H, NK, T), F32),
            jax.ShapeDtypeStruct((H, NK, T), F32),
            jax.ShapeDtypeStruct((H, NK // 2, T), U32),
            jax.ShapeDtypeStruct((H, NK // 2, T), U32),
        ],
        compiler_params=_params("parallel"),
        name="peer_select",
    )(x, g, wq, keys)


def _gelu(a):
    return a * 0.5 * (1.0 + lax.erf(a * (1.0 / math.sqrt(2.0))))


MXU_DIM = 256
STEP_GROUPS = 4


def _peer_dense_kernel(x_ref, hnt_ref, u_ref, vt_ref, n1_ref, e1_ref, r2_ref, e2_ref, g_ref, o_ref, acc,
                       *, final):
    j = pl.program_id(1)
    tt = x_ref.shape[0]
    span = STEP_GROUPS * PEER_NKEYS
    steps = 2 * u_ref.shape[0] // span

    @pl.when(j == 0)
    def _():
        acc[...] = jnp.zeros_like(acc)

    hnt = _unpack(hnt_ref[...])

    def scores(s):
        rows = slice(s * span // 2, (s + 1) * span // 2)
        return _dot(_unpack(u_ref[rows, :]), hnt)

    def gates(s, tok):
        out = [None] * STEP_GROUPS
        for hd in range(PEER_HEADS):
            r2 = _unpack(r2_ref[hd, :, tok])
            e2 = _unpack(e2_ref[hd, :, tok])
            for g in range(STEP_GROUPS):
                r = s * STEP_GROUPS + g
                nb = jnp.broadcast_to(n1_ref[hd, r:r + 1, tok], (PEER_NKEYS, LANES)).astype(BF16)
                eb = jnp.broadcast_to(e1_ref[hd, r:r + 1, tok], (PEER_NKEYS, LANES)).astype(BF16)
                prod = e2 * eb
                term = jnp.where(r2 < nb, prod, jnp.zeros_like(prod))
                out[g] = term if out[g] is None else out[g] + term
        return jnp.concatenate(out, axis=0)

    def weights(s, a):
        cols = []
        for t0 in range(0, tt, LANES):
            tok = slice(t0, t0 + LANES)
            cols.append(gates(s, tok) * _gelu(a[:, tok]).astype(BF16))
        return jnp.concatenate(cols, axis=1)

    a_next = scores(0)
    for s in range(steps):
        a_cur = a_next
        if s + 1 < steps:
            a_next = scores(s + 1)
        w = weights(s, a_cur)
        acc[...] += _dot(_unpack(vt_ref[:, s * span:(s + 1) * span]), w)

    @pl.when(j == pl.num_programs(1) - 1)
    def _():
        o = x_ref[...] + acc[...].T
        o_ref[...] = _rmsnorm_rows(o, g_ref[...]) if final else o


def peer_dense(x, hnt, u, vt, n1, e1, r2, e2, g, *, tt, ec, final):
    T, D = x.shape
    E = 2 * u.shape[0]
    H, NK = PEER_HEADS, PEER_NKEYS
    row_spec = pl.BlockSpec((H, ec // NK, tt), lambda i, j: (0, j, i))
    key_spec = pl.BlockSpec((H, NK // 2, tt), lambda i, j: (0, 0, i))
    return pl.pallas_call(
        functools.partial(_peer_dense_kernel, final=final),
        grid=(T // tt, E // ec),
        in_specs=[
            pl.BlockSpec((tt, D), lambda i, j: (i, 0)),
            pl.BlockSpec((D // 2, tt), lambda i, j: (0, i)),
            pl.BlockSpec((ec // 2, D), lambda i, j: (j, 0)),
            pl.BlockSpec((D // 2, ec), lambda i, j: (0, j)),
            row_spec, row_spec, key_spec, key_spec,
            pl.BlockSpec((1, D), lambda i, j: (0, 0)),
        ],
        out_specs=pl.BlockSpec((tt, D), lambda i, j: (i, 0)),
        out_shape=jax.ShapeDtypeStruct((T, D), F32),
        scratch_shapes=[pltpu.VMEM((D, tt), F32)],
        compiler_params=_params("parallel", "arbitrary"),
        name="peer_dense",
    )(x, hnt, u, vt, n1, e1, r2, e2, g)


def _pack_rows_kernel(w_ref, o_ref):
    o_ref[...] = _pack(w_ref[...].astype(BF16))


def pack_rows(w, layer, *, tr):
    _, R, C = w.shape
    return pl.pallas_call(
        _pack_rows_kernel,
        grid=(R // tr,),
        in_specs=[pl.BlockSpec((None, tr, C), lambda i: (layer, i, 0))],
        out_specs=pl.BlockSpec((tr // 2, C), lambda i: (i, 0)),
        out_shape=jax.ShapeDtypeStruct((R // 2, C), U32),
        compiler_params=_params("parallel"),
        name="pack_rows",
    )(w)


def _pack_transposed_kernel(w_ref, o_ref):
    o_ref[...] = _pack(w_ref[...].T.astype(BF16))


def pack_transposed(w, layer, *, tr):
    _, R, C = w.shape
    return pl.pallas_call(
        _pack_transposed_kernel,
        grid=(R // tr,),
        in_specs=[pl.BlockSpec((None, tr, C), lambda i: (layer, i, 0))],
        out_specs=pl.BlockSpec((C // 2, tr), lambda i: (0, i)),
        out_shape=jax.ShapeDtypeStruct((C // 2, R), U32),
        compiler_params=_params("parallel"),
        name="pack_transposed",
    )(w)


def _rope_tables(S):
    d = RET_HEAD_DIM
    inv = ROPE_BASE ** (-jnp.arange(0, d, 2, dtype=F32) / d)
    ang = jnp.arange(S, dtype=F32)[:, None] * inv[None, :]
    cos, sin = jnp.cos(ang), jnp.sin(ang)
    return jnp.concatenate([cos, cos], axis=1), jnp.concatenate([-sin, sin], axis=1)


def _tile(n, want):
    t = min(n, want)
    assert n % t == 0, (n, t)
    return t


def kernel(x, mem, mix_norm, w_in, sb_norm, ret_norm, w_out, xa_norm, mem_norm, xa_wq, xa_wkv, xa_wo,
           peer_norm, peer_wq, peer_keys, peer_u, peer_v, final_norm_g):
    B, S, D = x.shape
    M = mem.shape[1]
    T = B * S
    depth = w_in.shape[0]
    assert D == D_MODEL and S % SB_BLOCK == 0 and w_in.shape[2] == IN_COLS

    cos, sin = _rope_tables(S)
    log_gamma = jnp.log1p(-jnp.power(2.0, -5.0 - jnp.arange(RET_HEADS, dtype=F32)))
    log_gamma = jnp.broadcast_to(log_gamma[:, None, None], (RET_HEADS, 1, LANES))
    row = lambda v: v.reshape(1, -1).astype(F32)
    tm = _tile(T, 512)

    xt = x.reshape(T, D)
    mem2 = mem.reshape(B * M, D)
    for l in range(depth):
        proj = norm_matmul(xt, row(mix_norm[l]), w_in[l].astype(BF16), tm=tm, tn=IN_COLS // 7)
        proj = proj.reshape(B, S, IN_COLS)
        sb_o = sb_attention(proj, row(sb_norm[l]))
        ret_o = retention(proj, row(ret_norm[l]), cos, sin, log_gamma)
        w_o = w_out[l].astype(BF16)
        xt = out_proj(xt, sb_o.reshape(T, SB_WIDTH), ret_o.reshape(T, RET_WIDTH),
                      w_o[:SB_WIDTH], w_o[SB_WIDTH:], tm=tm)

        kv = norm_matmul(mem2, row(mem_norm[l]), xa_wkv[l].astype(BF16), tm=_tile(B * M, 512), tn=1024)
        xt = cross_attn(xt.reshape(B, S, D), row(xa_norm[l]), xa_wq[l].astype(BF16),
                        kv.reshape(B, M, 2 * D), xa_wo[l].astype(BF16), tq=_tile(S, 512)).reshape(T, D)

        hn, n1, e1, r2, e2 = peer_select(xt, row(peer_norm[l]), peer_wq[l].astype(BF16),
                                         peer_keys[l].astype(BF16), tt=_tile(T, 256))
        xt = peer_dense(xt, hn, pack_rows(peer_u, l, tr=1024), pack_transposed(peer_v, l, tr=512),
                        n1, e1, r2, e2, row(final_norm_g), tt=_tile(T, 512), ec=1024,
                        final=l == depth - 1)
    return xt.reshape(B, S, D)
```

```python
import functools
import math

import jax
import jax.numpy as jnp
from jax import lax
from jax.experimental import pallas as pl
from jax.experimental.pallas import tpu as pltpu

F32 = jnp.float32
BF16 = jnp.bfloat16
U32 = jnp.uint32

D_MODEL = 1024
SB_HEADS = 8
SB_HEAD_DIM = 64
SB_WIDTH = 512
SB_BLOCK = 128
RET_HEADS = 4
RET_HEAD_DIM = 128
RET_WIDTH = 512
RET_CHUNK = 128
ROPE_BASE = 10000.0
IN_COLS = 3 * SB_WIDTH + 4 * RET_WIDTH
XA_HEADS = 4
XA_HEAD_DIM = 256
PEER_HEADS = 8
PEER_NKEYS = 128
PEER_DQ = 256
PEER_TOPK = 16
EPS = 1e-6

LANES = 128
SUBLANES = 8
VMEM_LIMIT = 56 * 1024 * 1024
NEG_INF = float("-inf")
EXP_ZERO_BELOW = -88.0


def _params(*sem):
    return pltpu.CompilerParams(dimension_semantics=sem, vmem_limit_bytes=VMEM_LIMIT)


def _dot(a, b):
    return jnp.dot(a, b, preferred_element_type=F32)


def _dot_nt(a, b):
    return lax.dot_general(a, b, (((1,), (1,)), ((), ())), preferred_element_type=F32)


def _rmsnorm_rows(x, g):
    ms = jnp.mean(x * x, axis=-1, keepdims=True)
    return x * lax.rsqrt(ms + EPS) * g


def _unpack(words):
    return pltpu.bitcast(words, BF16)


def _pack(rows):
    return pltpu.bitcast(rows, U32)


def _norm_matmul_kernel(x_ref, g_ref, w_ref, o_ref, *, tn):
    h = _rmsnorm_rows(x_ref[...], g_ref[...]).astype(BF16)
    for n0 in range(0, w_ref.shape[1], tn):
        o_ref[:, n0:n0 + tn] = _dot(h, w_ref[:, n0:n0 + tn]).astype(o_ref.dtype)


def norm_matmul(x, g, w, *, tm, tn, out_dtype=BF16):
    T, D = x.shape
    N = w.shape[1]
    return pl.pallas_call(
        functools.partial(_norm_matmul_kernel, tn=tn),
        grid=(T // tm,),
        in_specs=[
            pl.BlockSpec((tm, D), lambda i: (i, 0)),
            pl.BlockSpec((1, D), lambda i: (0, 0)),
            pl.BlockSpec((D, N), lambda i: (0, 0)),
        ],
        out_specs=pl.BlockSpec((tm, N), lambda i: (i, 0)),
        out_shape=jax.ShapeDtypeStruct((T, N), out_dtype),
        compiler_params=_params("parallel"),
        name="norm_matmul",
    )(x, g, w)


def _softplus(z):
    return jnp.maximum(z, 0.0) + jnp.log(1.0 + jnp.exp(-jnp.abs(z)))


def _sb_kernel(q_ref, k_ref, v_ref, g_ref, o_ref, acc, run):
    blk = SB_BLOCK
    pairs = SB_WIDTH // LANES
    i = pl.program_id(1)
    scale = SB_HEAD_DIM ** -0.5
    first = lax.broadcasted_iota(jnp.int32, (1, LANES), 1) < SB_HEAD_DIM
    row = lax.broadcasted_iota(jnp.int32, (2 * blk, blk), 0) % blk
    col = lax.broadcasted_iota(jnp.int32, (2 * blk, blk), 1)
    causal = col < row
    r2 = lax.broadcasted_iota(jnp.int32, (2 * blk, 2 * blk), 0) % blk
    c2 = lax.broadcasted_iota(jnp.int32, (2 * blk, 2 * blk), 1)
    suffix = jnp.where((c2 >= blk) | (r2 > c2), 1.0, 0.0).astype(BF16)

    def split_heads(t):
        zero = jnp.zeros_like(t)
        return jnp.concatenate([jnp.where(first, t, zero), jnp.where(first, zero, t)], axis=0)

    def add_key_block(j, diagonal):
        rows = pl.ds(pl.multiple_of(j * blk, blk), blk)
        col_of = [slice(p * LANES, (p + 1) * LANES) for p in range(pairs)]
        zs = [_dot_nt(split_heads(q_ref[:, c]), k_ref[rows, c]) * scale for c in col_of]
        sps = [_softplus(z) for z in zs]
        sums = []
        for sp in sps:
            lnb = jnp.where(causal, -sp, 0.0) if diagonal else -sp
            hi = lnb.astype(BF16)
            lo = (lnb - hi.astype(F32)).astype(BF16)
            sums.append(_dot(jnp.concatenate([hi, lo], axis=1), suffix))
        outs = []
        for p in range(pairs):
            stick, tot = sums[p][:, :blk], sums[p][:, blk:]
            if diagonal:
                a = jnp.where(causal, jnp.exp(zs[p] - sps[p] + stick), 0.0)
                run[p] = tot
            else:
                a = jnp.exp(zs[p] - sps[p] + stick + run[p])
                run[p] += tot
            a = a.astype(BF16)
            outs.append(_dot(jnp.concatenate([a[:blk], a[blk:]], axis=1),
                             split_heads(v_ref[rows, col_of[p]])))
        for p in range(pairs):
            acc[p] = outs[p] if diagonal else acc[p] + outs[p]

    def highest_run():
        m = run[0]
        for p in range(1, pairs):
            m = jnp.maximum(m, run[p])
        return jnp.max(m)

    add_key_block(i, True)

    def cond(carry):
        jj, top = carry
        return (jj <= i) & (top > EXP_ZERO_BELOW)

    def body(carry):
        jj, _ = carry
        add_key_block(i - jj, False)
        return jj + 1, highest_run()

    lax.while_loop(cond, body, (jnp.int32(1), highest_run()))

    for p in range(pairs):
        cols = slice(p * LANES, (p + 1) * LANES)
        o = acc[p]
        sq = o * o
        ms0 = jnp.sum(jnp.where(first, sq, 0.0), axis=-1, keepdims=True)
        ms1 = jnp.sum(jnp.where(first, 0.0, sq), axis=-1, keepdims=True)
        ms = jnp.where(first, ms0, ms1) * (1.0 / SB_HEAD_DIM)
        o_ref[:, cols] = (o * lax.rsqrt(ms + EPS) * g_ref[:, cols]).astype(o_ref.dtype)


def sb_attention(proj, sb_norm):
    B, S, _ = proj.shape
    pairs = SB_WIDTH // LANES
    return pl.pallas_call(
        _sb_kernel,
        grid=(B, S // SB_BLOCK),
        in_specs=[
            pl.BlockSpec((None, SB_BLOCK, SB_WIDTH), lambda b, i: (b, i, 0)),
            pl.BlockSpec((None, S, SB_WIDTH), lambda b, i: (b, 0, 1)),
            pl.BlockSpec((None, S, SB_WIDTH), lambda b, i: (b, 0, 2)),
            pl.BlockSpec((1, SB_WIDTH), lambda b, i: (0, 0)),
        ],
        out_specs=pl.BlockSpec((None, SB_BLOCK, SB_WIDTH), lambda b, i: (b, i, 0)),
        out_shape=jax.ShapeDtypeStruct((B, S, SB_WIDTH), BF16),
        scratch_shapes=[pltpu.VMEM((pairs, SB_BLOCK, LANES), F32),
                        pltpu.VMEM((pairs, 2 * SB_BLOCK, LANES), F32)],
        compiler_params=_params("parallel", "arbitrary"),
        name="sb_attention",
    )(proj, proj, proj, sb_norm)


def _retention_kernel(q_ref, k_ref, v_ref, gate_ref, cos_ref, sin_ref, lg_ref, g_ref, o_ref, state):
    C = RET_CHUNK
    n = pl.program_id(1)

    @pl.when(n == 0)
    def _():
        state[...] = jnp.zeros_like(state)

    cos, sin = cos_ref[...], sin_ref[...]

    def rope(t):
        t = t.astype(F32)
        return t * cos + pltpu.roll(t, RET_HEAD_DIM // 2, 1) * sin

    row = lax.broadcasted_iota(jnp.int32, (C, C), 0).astype(F32)
    col = lax.broadcasted_iota(jnp.int32, (C, C), 1).astype(F32)
    diff = row - col
    heads = range(RET_HEADS)
    col_of = [slice(hd * RET_HEAD_DIM, (hd + 1) * RET_HEAD_DIM) for hd in heads]
    lgs = [lg_ref[hd] for hd in heads]
    qcs = [rope(q_ref[:, c]) for c in col_of]
    kcs = [rope(k_ref[:, c]) * RET_HEAD_DIM ** -0.5 for c in col_of]
    inner = [_dot_nt(qcs[hd].astype(BF16), kcs[hd].astype(BF16))
             * jnp.where(diff >= 0, jnp.exp(diff * lgs[hd]), 0.0) for hd in heads]
    sts = [state[hd] for hd in heads]
    outs = [_dot(inner[hd].astype(BF16), v_ref[:, col_of[hd]])
            + _dot((qcs[hd] * jnp.exp((row + 1) * lgs[hd])).astype(BF16), sts[hd].astype(BF16))
            for hd in heads]
    for hd in heads:
        zeta = jnp.exp((C - 1 - row) * lgs[hd])
        state[hd] = jnp.exp(C * lgs[hd]) * sts[hd] + _dot((kcs[hd] * zeta).T.astype(BF16),
                                                          v_ref[:, col_of[hd]])
    for hd in heads:
        o = outs[hd]
        mu = jnp.mean(o, axis=-1, keepdims=True)
        var = jnp.mean((o - mu) ** 2, axis=-1, keepdims=True)
        y = (o - mu) * lax.rsqrt(var + EPS) * g_ref[:, col_of[hd]]
        gate = gate_ref[:, col_of[hd]].astype(F32)
        o_ref[:, col_of[hd]] = (y * (gate * jax.nn.sigmoid(gate))).astype(o_ref.dtype)


def retention(proj, ret_norm, cos, sin, log_gamma):
    B, S, _ = proj.shape
    C = RET_CHUNK
    base = 3 * SB_WIDTH // RET_WIDTH

    def col(group):
        return pl.BlockSpec((None, C, RET_WIDTH), lambda b, n: (b, n, base + group))

    return pl.pallas_call(
        _retention_kernel,
        grid=(B, S // C),
        in_specs=[
            col(0), col(1), col(2), col(3),
            pl.BlockSpec((C, LANES), lambda b, n: (n, 0)),
            pl.BlockSpec((C, LANES), lambda b, n: (n, 0)),
            pl.BlockSpec((RET_HEADS, 1, LANES), lambda b, n: (0, 0, 0)),
            pl.BlockSpec((1, RET_WIDTH), lambda b, n: (0, 0)),
        ],
        out_specs=pl.BlockSpec((None, C, RET_WIDTH), lambda b, n: (b, n, 0)),
        out_shape=jax.ShapeDtypeStruct((B, S, RET_WIDTH), BF16),
        scratch_shapes=[pltpu.VMEM((RET_HEADS, RET_HEAD_DIM, RET_HEAD_DIM), F32)],
        compiler_params=_params("parallel", "arbitrary"),
        name="retention",
    )(proj, proj, proj, proj, cos, sin, log_gamma, ret_norm)


def _cross_attn_kernel(x_ref, a_ref, b_ref, wa_ref, wb_ref, g_ref, wq_ref, kv_ref, wo_ref, o_ref):
    x = x_ref[...] + _dot(a_ref[...], wa_ref[...]) + _dot(b_ref[...], wb_ref[...])
    h = _rmsnorm_rows(x, g_ref[...]).astype(BF16)
    q = _dot(h, wq_ref[...]).astype(BF16)
    outs = []
    for hd in range(XA_HEADS):
        lo = hd * XA_HEAD_DIM
        k = kv_ref[:, lo:lo + XA_HEAD_DIM]
        v = kv_ref[:, D_MODEL + lo:D_MODEL + lo + XA_HEAD_DIM]
        s = _dot_nt(q[:, lo:lo + XA_HEAD_DIM], k) * XA_HEAD_DIM ** -0.5
        e = jnp.exp(s - jnp.max(s, axis=-1, keepdims=True))
        p = e / jnp.sum(e, axis=-1, keepdims=True)
        outs.append(_dot(p.astype(BF16), v).astype(BF16))
    o_ref[...] = x + _dot(jnp.concatenate(outs, axis=1), wo_ref[...])


def cross_attn(x, a, b, wa, wb, g, wq, kv, wo, *, tq):
    B, S, D = x.shape
    M = kv.shape[1]
    const = lambda b, i: (0, 0)
    tile = lambda w: pl.BlockSpec((None, tq, w), lambda b, i: (b, i, 0))
    return pl.pallas_call(
        _cross_attn_kernel,
        grid=(B, S // tq),
        in_specs=[
            tile(D), tile(a.shape[2]), tile(b.shape[2]),
            pl.BlockSpec(wa.shape, const),
            pl.BlockSpec(wb.shape, const),
            pl.BlockSpec((1, D), const),
            pl.BlockSpec((D, D), const),
            pl.BlockSpec((None, M, 2 * D), lambda b, i: (b, 0, 0)),
            pl.BlockSpec((D, D), const),
        ],
        out_specs=tile(D),
        out_shape=jax.ShapeDtypeStruct((B, S, D), F32),
        compiler_params=_params("parallel", "arbitrary"),
        name="cross_attn",
    )(x, a, b, wa, wb, g, wq, kv, wo)


def _kth_value(s, k):
    for _ in range(k - 1):
        s = jnp.where(s == jnp.max(s, axis=0, keepdims=True), NEG_INF, s)
    return jnp.max(s, axis=0, keepdims=True)


INT32_MIN = -2 ** 31
INT32_MAX = 2 ** 31 - 1


def _order_key(v):
    b = pltpu.bitcast(v, jnp.int32) if v.dtype == F32 else v
    k = b ^ ((b >> 31) & jnp.int32(INT32_MAX))
    return k if v.dtype == F32 else pltpu.bitcast(k, F32)


def _top_values(s, k):
    rowid = lax.broadcasted_iota(jnp.int32, (k, s.shape[1]), 0)
    key = _order_key(s)
    vals, stacked = [], jnp.full((k, s.shape[1]), INT32_MIN, jnp.int32)
    for r in range(k):
        m = jnp.max(key, axis=0, keepdims=True)
        vals.append(_order_key(m))
        stacked = jnp.where(rowid == r, m, stacked)
        key = jnp.where(key == m, jnp.int32(INT32_MIN + r), key)
    rank = jnp.where(key < jnp.int32(INT32_MIN + k), key & jnp.int32(INT32_MAX), k).astype(F32)
    return vals, _order_key(stacked), rank


def _pair_candidates(t1, t1s, t2, t2s):
    K, sub = PEER_TOPK, 8
    rowid = lax.broadcasted_iota(jnp.int32, (sub, t1s.shape[1]), 0)
    pieces = [t1[0] + t2s]
    for a in range(1, sub):
        pieces.append(jnp.where(rowid < K // (a + 1), t1[a] + t2s[:sub], NEG_INF))
    pieces.append(t1s[sub:] + t2[0])
    return jnp.concatenate(pieces, axis=0)


def _peer_select_kernel(x_ref, g_ref, wq_ref, keys_ref, hnt_ref, n1_ref, e1_ref, r2_ref, e2_ref):
    K = PEER_TOPK
    hf = _rmsnorm_rows(x_ref[...], g_ref[...])
    hnt_ref[...] = _pack(hf.T.astype(BF16))
    q = _dot(hf.astype(BF16), wq_ref[...]).astype(BF16)
    half = PEER_DQ // 2
    for hd in range(PEER_HEADS):
        s1 = _dot_nt(keys_ref[hd, 0], q[:, hd * PEER_DQ:hd * PEER_DQ + half])
        s2 = _dot_nt(keys_ref[hd, 1], q[:, hd * PEER_DQ + half:(hd + 1) * PEER_DQ])
        t1, t1s, rank1 = _top_values(s1, K)
        t2, t2s, rank2 = _top_values(s2, K)
        cand = _pair_candidates(t1, t1s, t2, t2s)
        tau = _kth_value(cand, K)
        z = jnp.sum(jnp.where(cand >= tau, jnp.exp(cand - (t1[0] + t2[0])), 0.0), axis=0, keepdims=True)
        rank1_b = rank1.astype(BF16)
        n1 = jnp.zeros_like(rank1_b)
        for a in range(K):
            n_a = jnp.sum(jnp.where(t1[a] + t2s >= tau, 1.0, 0.0), axis=0, keepdims=True)
            n1 = jnp.where(rank1_b == float(a), jnp.broadcast_to(n_a, s1.shape).astype(BF16), n1)
        n1_ref[hd] = n1.astype(F32)
        e1_ref[hd] = jnp.where(rank1 < float(K), jnp.exp(s1 - t1[0]), 0.0)
        r2_ref[hd] = _pack(rank2.astype(BF16))
        e2_ref[hd] = _pack((jnp.exp(s2 - t2[0]) / z).astype(BF16))


def peer_select(x, g, wq, keys, *, tt):
    T, D = x.shape
    H, NK = PEER_HEADS, PEER_NKEYS
    const2 = lambda i: (0, 0)
    row_spec = pl.BlockSpec((H, NK, tt), lambda i: (0, 0, i))
    key_spec = pl.BlockSpec((H, NK // 2, tt), lambda i: (0, 0, i))
    return pl.pallas_call(
        _peer_select_kernel,
        grid=(T // tt,),
        in_specs=[
            pl.BlockSpec((tt, D), lambda i: (i, 0)),
            pl.BlockSpec((1, D), const2),
            pl.BlockSpec(wq.shape, const2),
            pl.BlockSpec(keys.shape, lambda i: (0, 0, 0, 0)),
        ],
        out_specs=[pl.BlockSpec((D // 2, tt), lambda i: (0, i)), row_spec, row_spec, key_spec, key_spec],
        out_shape=[
            jax.ShapeDtypeStruct((D // 2, T), U32),
            jax.ShapeDtypeStruct((H, NK, T), F32),
            jax.ShapeDtypeStruct((H, NK, T), F32),
            jax.ShapeDtypeStruct((H, NK // 2, T), U32),
            jax.ShapeDtypeStruct((H, NK // 2, T), U32),
        ],
        compiler_params=_params("parallel"),
        name="peer_select",
    )(x, g, wq, keys)


def _gelu(a):
    return a * 0.5 * (1.0 + lax.erf(a * (1.0 / math.sqrt(2.0))))


MXU_DIM = 256
STEP_GROUPS = (4, 4)


def _peer_dense_kernel(x_ref, hnt_ref, u_ref, vt_ref, n1_ref, e1_ref, r2_ref, e2_ref, g_ref, o_ref, acc,
                       *, final):
    j = pl.program_id(1)
    tt = x_ref.shape[0]
    assert sum(STEP_GROUPS) * PEER_NKEYS == 2 * u_ref.shape[0]
    first = [sum(STEP_GROUPS[:s]) for s in range(len(STEP_GROUPS) + 1)]

    @pl.when(j == 0)
    def _():
        acc[...] = jnp.zeros_like(acc)

    hnt = _unpack(hnt_ref[...])

    def scores(s):
        rows = slice(first[s] * PEER_NKEYS // 2, first[s + 1] * PEER_NKEYS // 2)
        return _dot(_unpack(u_ref[rows, :]), hnt)

    def gates(s, tok):
        out = [None] * STEP_GROUPS[s]
        for hd in range(PEER_HEADS):
            r2 = _unpack(r2_ref[hd, :, tok])
            e2 = _unpack(e2_ref[hd, :, tok])
            for g in range(STEP_GROUPS[s]):
                r = first[s] + g
                nb = jnp.broadcast_to(n1_ref[hd, r:r + 1, tok], (PEER_NKEYS, LANES)).astype(BF16)
                eb = jnp.broadcast_to(e1_ref[hd, r:r + 1, tok], (PEER_NKEYS, LANES)).astype(BF16)
                prod = e2 * eb
                term = jnp.where(r2 < nb, prod, jnp.zeros_like(prod))
                out[g] = term if out[g] is None else out[g] + term
        return jnp.concatenate(out, axis=0)

    def weights(s, a):
        cols = []
        for t0 in range(0, tt, LANES):
            tok = slice(t0, t0 + LANES)
            cols.append(gates(s, tok) * _gelu(a[:, tok]).astype(BF16))
        return jnp.concatenate(cols, axis=1)

    a_next = scores(0)
    for s in range(len(STEP_GROUPS)):
        a_cur = a_next
        if s + 1 < len(STEP_GROUPS):
            a_next = scores(s + 1)
        w = weights(s, a_cur)
        cols = slice(first[s] * PEER_NKEYS, first[s + 1] * PEER_NKEYS)
        acc[...] += _dot(_unpack(vt_ref[:, cols]), w)

    @pl.when(j == pl.num_programs(1) - 1)
    def _():
        o = x_ref[...] + acc[...].T
        o_ref[...] = _rmsnorm_rows(o, g_ref[...]) if final else o


def peer_dense(x, hnt, u, vt, n1, e1, r2, e2, g, *, tt, ec, final):
    T, D = x.shape
    E = 2 * u.shape[0]
    H, NK = PEER_HEADS, PEER_NKEYS
    row_spec = pl.BlockSpec((H, ec // NK, tt), lambda i, j: (0, j, i))
    key_spec = pl.BlockSpec((H, NK // 2, tt), lambda i, j: (0, 0, i))
    return pl.pallas_call(
        functools.partial(_peer_dense_kernel, final=final),
        grid=(T // tt, E // ec),
        in_specs=[
            pl.BlockSpec((tt, D), lambda i, j: (i, 0)),
            pl.BlockSpec((D // 2, tt), lambda i, j: (0, i)),
            pl.BlockSpec((ec // 2, D), lambda i, j: (j, 0)),
            pl.BlockSpec((D // 2, ec), lambda i, j: (0, j)),
            row_spec, row_spec, key_spec, key_spec,
            pl.BlockSpec((1, D), lambda i, j: (0, 0)),
        ],
        out_specs=pl.BlockSpec((tt, D), lambda i, j: (i, 0)),
        out_shape=jax.ShapeDtypeStruct((T, D), F32),
        scratch_shapes=[pltpu.VMEM((D, tt), F32)],
        compiler_params=_params("parallel", "arbitrary"),
        name="peer_dense",
    )(x, hnt, u, vt, n1, e1, r2, e2, g)


def _pack_rows_kernel(w_ref, o_ref):
    o_ref[...] = _pack(w_ref[...].astype(BF16))


def pack_rows(w, layer, *, tr):
    _, R, C = w.shape
    return pl.pallas_call(
        _pack_rows_kernel,
        grid=(R // tr,),
        in_specs=[pl.BlockSpec((None, tr, C), lambda i: (layer, i, 0))],
        out_specs=pl.BlockSpec((tr // 2, C), lambda i: (i, 0)),
        out_shape=jax.ShapeDtypeStruct((R // 2, C), U32),
        compiler_params=_params("parallel"),
        name="pack_rows",
    )(w)


def _pack_transposed_kernel(w_ref, o_ref):
    o_ref[...] = _pack(w_ref[...].T.astype(BF16))


def pack_transposed(w, layer, *, tr):
    _, R, C = w.shape
    return pl.pallas_call(
        _pack_transposed_kernel,
        grid=(R // tr,),
        in_specs=[pl.BlockSpec((None, tr, C), lambda i: (layer, i, 0))],
        out_specs=pl.BlockSpec((C // 2, tr), lambda i: (0, i)),
        out_shape=jax.ShapeDtypeStruct((C // 2, R), U32),
        compiler_params=_params("parallel"),
        name="pack_transposed",
    )(w)


def _rope_tables(S):
    d = RET_HEAD_DIM
    inv = ROPE_BASE ** (-jnp.arange(0, d, 2, dtype=F32) / d)
    ang = jnp.arange(S, dtype=F32)[:, None] * inv[None, :]
    cos, sin = jnp.cos(ang), jnp.sin(ang)
    return jnp.concatenate([cos, cos], axis=1), jnp.concatenate([-sin, sin], axis=1)


def _tile(n, want):
    t = min(n, want)
    assert n % t == 0, (n, t)
    return t


def kernel(x, mem, mix_norm, w_in, sb_norm, ret_norm, w_out, xa_norm, mem_norm, xa_wq, xa_wkv, xa_wo,
           peer_norm, peer_wq, peer_keys, peer_u, peer_v, final_norm_g):
    B, S, D = x.shape
    M = mem.shape[1]
    T = B * S
    depth = w_in.shape[0]
    assert D == D_MODEL and S % SB_BLOCK == 0 and w_in.shape[2] == IN_COLS

    cos, sin = _rope_tables(S)
    log_gamma = jnp.log1p(-jnp.power(2.0, -5.0 - jnp.arange(RET_HEADS, dtype=F32)))
    log_gamma = jnp.broadcast_to(log_gamma[:, None, None], (RET_HEADS, 1, LANES))
    row = lambda v: v.reshape(1, -1).astype(F32)
    tm = _tile(T, 512)

    xt = x.reshape(T, D)
    mem2 = mem.reshape(B * M, D)
    for l in range(depth):
        proj = norm_matmul(xt, row(mix_norm[l]), w_in[l].astype(BF16), tm=tm, tn=IN_COLS // 7)
        proj = proj.reshape(B, S, IN_COLS)
        sb_o = sb_attention(proj, row(sb_norm[l]))
        ret_o = retention(proj, row(ret_norm[l]), cos, sin, log_gamma)
        w_o = w_out[l].astype(BF16)
        kv = norm_matmul(mem2, row(mem_norm[l]), xa_wkv[l].astype(BF16), tm=_tile(B * M, 512), tn=1024)
        xt = cross_attn(xt.reshape(B, S, D), sb_o, ret_o, w_o[:SB_WIDTH], w_o[SB_WIDTH:],
                        row(xa_norm[l]), xa_wq[l].astype(BF16), kv.reshape(B, M, 2 * D),
                        xa_wo[l].astype(BF16), tq=_tile(S, 512)).reshape(T, D)

        hn, n1, e1, r2, e2 = peer_select(xt, row(peer_norm[l]), peer_wq[l].astype(BF16),
                                         peer_keys[l].astype(BF16), tt=_tile(T, 256))
        xt = peer_dense(xt, hn, pack_rows(peer_u, l, tr=1024), pack_transposed(peer_v, l, tr=512),
                        n1, e1, r2, e2, row(final_norm_g), tt=_tile(T, 1024), ec=1024,
                        final=l == depth - 1)
    return xt.reshape(B, S, D)
```

```python
import functools
import math

import jax
import jax.numpy as jnp
from jax import lax
from jax.experimental import pallas as pl
from jax.experimental.pallas import tpu as pltpu

F32 = jnp.float32
BF16 = jnp.bfloat16
U32 = jnp.uint32

D_MODEL = 1024
SB_HEADS = 8
SB_HEAD_DIM = 64
SB_WIDTH = 512
SB_BLOCK = 128
RET_HEADS = 4
RET_HEAD_DIM = 128
RET_WIDTH = 512
RET_CHUNK = 128
ROPE_BASE = 10000.0
IN_COLS = 3 * SB_WIDTH + 4 * RET_WIDTH
XA_HEADS = 4
XA_HEAD_DIM = 256
PEER_HEADS = 8
PEER_NKEYS = 128
PEER_DQ = 256
PEER_TOPK = 16
EPS = 1e-6

LANES = 128
SUBLANES = 8
VMEM_LIMIT = 56 * 1024 * 1024
NEG_INF = float("-inf")
EXP_ZERO_BELOW = -88.0


def _params(*sem):
    return pltpu.CompilerParams(dimension_semantics=sem, vmem_limit_bytes=VMEM_LIMIT)


def _dot(a, b):
    return jnp.dot(a, b, preferred_element_type=F32)


def _dot_nt(a, b):
    return lax.dot_general(a, b, (((1,), (1,)), ((), ())), preferred_element_type=F32)


def _rmsnorm_rows(x, g):
    ms = jnp.mean(x * x, axis=-1, keepdims=True)
    return x * lax.rsqrt(ms + EPS) * g


def _unpack(words):
    return pltpu.bitcast(words, BF16)


def _pack(rows):
    return pltpu.bitcast(rows, U32)


def _norm_matmul_kernel(x_ref, g_ref, w_ref, o_ref, *, tn):
    h = _rmsnorm_rows(x_ref[...], g_ref[...]).astype(BF16)
    for n0 in range(0, w_ref.shape[1], tn):
        o_ref[:, n0:n0 + tn] = _dot(h, w_ref[:, n0:n0 + tn]).astype(o_ref.dtype)


def norm_matmul(x, g, w, *, tm, tn, out_dtype=BF16):
    T, D = x.shape
    N = w.shape[1]
    return pl.pallas_call(
        functools.partial(_norm_matmul_kernel, tn=tn),
        grid=(T // tm,),
        in_specs=[
            pl.BlockSpec((tm, D), lambda i: (i, 0)),
            pl.BlockSpec((1, D), lambda i: (0, 0)),
            pl.BlockSpec((D, N), lambda i: (0, 0)),
        ],
        out_specs=pl.BlockSpec((tm, N), lambda i: (i, 0)),
        out_shape=jax.ShapeDtypeStruct((T, N), out_dtype),
        compiler_params=_params("parallel"),
        name="norm_matmul",
    )(x, g, w)


def _softplus(z):
    return jnp.maximum(z, 0.0) + jnp.log(1.0 + jnp.exp(-jnp.abs(z)))


def _sb_kernel(q_ref, k_ref, v_ref, g_ref, o_ref, acc, run):
    blk = SB_BLOCK
    pairs = SB_WIDTH // LANES
    i = pl.program_id(1)
    scale = SB_HEAD_DIM ** -0.5
    first = lax.broadcasted_iota(jnp.int32, (1, LANES), 1) < SB_HEAD_DIM
    row = lax.broadcasted_iota(jnp.int32, (2 * blk, blk), 0) % blk
    col = lax.broadcasted_iota(jnp.int32, (2 * blk, blk), 1)
    causal = col < row
    r2 = lax.broadcasted_iota(jnp.int32, (2 * blk, 2 * blk), 0) % blk
    c2 = lax.broadcasted_iota(jnp.int32, (2 * blk, 2 * blk), 1)
    suffix = jnp.where((c2 >= blk) | (r2 > c2), 1.0, 0.0).astype(BF16)

    def split_heads(t):
        zero = jnp.zeros_like(t)
        return jnp.concatenate([jnp.where(first, t, zero), jnp.where(first, zero, t)], axis=0)

    def add_key_block(j, diagonal):
        rows = pl.ds(pl.multiple_of(j * blk, blk), blk)
        col_of = [slice(p * LANES, (p + 1) * LANES) for p in range(pairs)]
        zs = [_dot_nt(split_heads(q_ref[:, c]), k_ref[rows, c]) * scale for c in col_of]
        sps = [_softplus(z) for z in zs]
        sums = []
        for sp in sps:
            lnb = jnp.where(causal, -sp, 0.0) if diagonal else -sp
            hi = lnb.astype(BF16)
            lo = (lnb - hi.astype(F32)).astype(BF16)
            sums.append(_dot(jnp.concatenate([hi, lo], axis=1), suffix))
        outs = []
        for p in range(pairs):
            stick, tot = sums[p][:, :blk], sums[p][:, blk:]
            if diagonal:
                a = jnp.where(causal, jnp.exp(zs[p] - sps[p] + stick), 0.0)
                run[p] = tot
            else:
                a = jnp.exp(zs[p] - sps[p] + stick + run[p])
                run[p] += tot
            a = a.astype(BF16)
            outs.append(_dot(jnp.concatenate([a[:blk], a[blk:]], axis=1),
                             split_heads(v_ref[rows, col_of[p]])))
        for p in range(pairs):
            acc[p] = outs[p] if diagonal else acc[p] + outs[p]

    def highest_run():
        m = run[0]
        for p in range(1, pairs):
            m = jnp.maximum(m, run[p])
        return jnp.max(m)

    add_key_block(i, True)

    def cond(carry):
        jj, top = carry
        return (jj <= i) & (top > EXP_ZERO_BELOW)

    def body(carry):
        jj, _ = carry
        add_key_block(i - jj, False)
        return jj + 1, highest_run()

    lax.while_loop(cond, body, (jnp.int32(1), highest_run()))

    for p in range(pairs):
        cols = slice(p * LANES, (p + 1) * LANES)
        o = acc[p]
        sq = o * o
        ms0 = jnp.sum(jnp.where(first, sq, 0.0), axis=-1, keepdims=True)
        ms1 = jnp.sum(jnp.where(first, 0.0, sq), axis=-1, keepdims=True)
        ms = jnp.where(first, ms0, ms1) * (1.0 / SB_HEAD_DIM)
        o_ref[:, cols] = (o * lax.rsqrt(ms + EPS) * g_ref[:, cols]).astype(o_ref.dtype)


def sb_attention(proj, sb_norm):
    B, S, _ = proj.shape
    pairs = SB_WIDTH // LANES
    return pl.pallas_call(
        _sb_kernel,
        grid=(B, S // SB_BLOCK),
        in_specs=[
            pl.BlockSpec((None, SB_BLOCK, SB_WIDTH), lambda b, i: (b, i, 0)),
            pl.BlockSpec((None, S, SB_WIDTH), lambda b, i: (b, 0, 1)),
            pl.BlockSpec((None, S, SB_WIDTH), lambda b, i: (b, 0, 2)),
            pl.BlockSpec((1, SB_WIDTH), lambda b, i: (0, 0)),
        ],
        out_specs=pl.BlockSpec((None, SB_BLOCK, SB_WIDTH), lambda b, i: (b, i, 0)),
        out_shape=jax.ShapeDtypeStruct((B, S, SB_WIDTH), BF16),
        scratch_shapes=[pltpu.VMEM((pairs, SB_BLOCK, LANES), F32),
                        pltpu.VMEM((pairs, 2 * SB_BLOCK, LANES), F32)],
        compiler_params=_params("parallel", "arbitrary"),
        name="sb_attention",
    )(proj, proj, proj, sb_norm)


def _retention_kernel(q_ref, k_ref, v_ref, gate_ref, cos_ref, sin_ref, lg_ref, g_ref, o_ref, state):
    C = RET_CHUNK
    n = pl.program_id(1)

    @pl.when(n == 0)
    def _():
        state[...] = jnp.zeros_like(state)

    cos, sin = cos_ref[...], sin_ref[...]

    def rope(t):
        t = t.astype(F32)
        return t * cos + pltpu.roll(t, RET_HEAD_DIM // 2, 1) * sin

    row = lax.broadcasted_iota(jnp.int32, (C, C), 0).astype(F32)
    col = lax.broadcasted_iota(jnp.int32, (C, C), 1).astype(F32)
    diff = row - col
    heads = range(RET_HEADS)
    col_of = [slice(hd * RET_HEAD_DIM, (hd + 1) * RET_HEAD_DIM) for hd in heads]
    lgs = [lg_ref[hd] for hd in heads]
    qcs = [rope(q_ref[:, c]) for c in col_of]
    kcs = [rope(k_ref[:, c]) * RET_HEAD_DIM ** -0.5 for c in col_of]
    inner = [_dot_nt(qcs[hd].astype(BF16), kcs[hd].astype(BF16))
             * jnp.where(diff >= 0, jnp.exp(diff * lgs[hd]), 0.0) for hd in heads]
    sts = [state[hd] for hd in heads]
    outs = [_dot(inner[hd].astype(BF16), v_ref[:, col_of[hd]])
            + _dot((qcs[hd] * jnp.exp((row + 1) * lgs[hd])).astype(BF16), sts[hd].astype(BF16))
            for hd in heads]
    for hd in heads:
        zeta = jnp.exp((C - 1 - row) * lgs[hd])
        state[hd] = jnp.exp(C * lgs[hd]) * sts[hd] + _dot((kcs[hd] * zeta).T.astype(BF16),
                                                          v_ref[:, col_of[hd]])
    for hd in heads:
        o = outs[hd]
        mu = jnp.mean(o, axis=-1, keepdims=True)
        var = jnp.mean((o - mu) ** 2, axis=-1, keepdims=True)
        y = (o - mu) * lax.rsqrt(var + EPS) * g_ref[:, col_of[hd]]
        gate = gate_ref[:, col_of[hd]].astype(F32)
        o_ref[:, col_of[hd]] = (y * (gate * jax.nn.sigmoid(gate))).astype(o_ref.dtype)


def retention(proj, ret_norm, cos, sin, log_gamma):
    B, S, _ = proj.shape
    C = RET_CHUNK
    base = 3 * SB_WIDTH // RET_WIDTH

    def col(group):
        return pl.BlockSpec((None, C, RET_WIDTH), lambda b, n: (b, n, base + group))

    return pl.pallas_call(
        _retention_kernel,
        grid=(B, S // C),
        in_specs=[
            col(0), col(1), col(2), col(3),
            pl.BlockSpec((C, LANES), lambda b, n: (n, 0)),
            pl.BlockSpec((C, LANES), lambda b, n: (n, 0)),
            pl.BlockSpec((RET_HEADS, 1, LANES), lambda b, n: (0, 0, 0)),
            pl.BlockSpec((1, RET_WIDTH), lambda b, n: (0, 0)),
        ],
        out_specs=pl.BlockSpec((None, C, RET_WIDTH), lambda b, n: (b, n, 0)),
        out_shape=jax.ShapeDtypeStruct((B, S, RET_WIDTH), BF16),
        scratch_shapes=[pltpu.VMEM((RET_HEADS, RET_HEAD_DIM, RET_HEAD_DIM), F32)],
        compiler_params=_params("parallel", "arbitrary"),
        name="retention",
    )(proj, proj, proj, proj, cos, sin, log_gamma, ret_norm)


def _cross_attn_kernel(x_ref, a_ref, b_ref, wa_ref, wb_ref, g_ref, wq_ref, kv_ref, wo_ref, o_ref):
    x = x_ref[...] + _dot(a_ref[...], wa_ref[...]) + _dot(b_ref[...], wb_ref[...])
    h = _rmsnorm_rows(x, g_ref[...]).astype(BF16)
    q = _dot(h, wq_ref[...]).astype(BF16)
    outs = []
    for hd in range(XA_HEADS):
        lo = hd * XA_HEAD_DIM
        k = kv_ref[:, lo:lo + XA_HEAD_DIM]
        v = kv_ref[:, D_MODEL + lo:D_MODEL + lo + XA_HEAD_DIM]
        s = _dot_nt(q[:, lo:lo + XA_HEAD_DIM], k) * XA_HEAD_DIM ** -0.5
        e = jnp.exp(s - jnp.max(s, axis=-1, keepdims=True))
        p = e / jnp.sum(e, axis=-1, keepdims=True)
        outs.append(_dot(p.astype(BF16), v).astype(BF16))
    o_ref[...] = x + _dot(jnp.concatenate(outs, axis=1), wo_ref[...])


def cross_attn(x, a, b, wa, wb, g, wq, kv, wo, *, tq):
    B, S, D = x.shape
    M = kv.shape[1]
    const = lambda b, i: (0, 0)
    tile = lambda w: pl.BlockSpec((None, tq, w), lambda b, i: (b, i, 0))
    return pl.pallas_call(
        _cross_attn_kernel,
        grid=(B, S // tq),
        in_specs=[
            tile(D), tile(a.shape[2]), tile(b.shape[2]),
            pl.BlockSpec(wa.shape, const),
            pl.BlockSpec(wb.shape, const),
            pl.BlockSpec((1, D), const),
            pl.BlockSpec((D, D), const),
            pl.BlockSpec((None, M, 2 * D), lambda b, i: (b, 0, 0)),
            pl.BlockSpec((D, D), const),
        ],
        out_specs=tile(D),
        out_shape=jax.ShapeDtypeStruct((B, S, D), F32),
        compiler_params=_params("parallel", "arbitrary"),
        name="cross_attn",
    )(x, a, b, wa, wb, g, wq, kv, wo)


def _kth_value(s, k):
    for _ in range(k - 1):
        s = jnp.where(s == jnp.max(s, axis=0, keepdims=True), NEG_INF, s)
    return jnp.max(s, axis=0, keepdims=True)


INT32_MIN = -2 ** 31
INT32_MAX = 2 ** 31 - 1


def _order_key(v):
    b = pltpu.bitcast(v, jnp.int32) if v.dtype == F32 else v
    k = b ^ ((b >> 31) & jnp.int32(INT32_MAX))
    return k if v.dtype == F32 else pltpu.bitcast(k, F32)


def _top_values(s, k):
    rowid = lax.broadcasted_iota(jnp.int32, (k, s.shape[1]), 0)
    key = _order_key(s)
    vals, stacked = [], jnp.full((k, s.shape[1]), INT32_MIN, jnp.int32)
    for r in range(k):
        m = jnp.max(key, axis=0, keepdims=True)
        vals.append(_order_key(m))
        stacked = jnp.where(rowid == r, m, stacked)
        key = jnp.where(key == m, jnp.int32(INT32_MIN + r), key)
    rank = jnp.where(key < jnp.int32(INT32_MIN + k), key & jnp.int32(INT32_MAX), k).astype(F32)
    return vals, _order_key(stacked), rank


def _pair_candidates(t1, t1s, t2, t2s):
    K, sub = PEER_TOPK, 8
    rowid = lax.broadcasted_iota(jnp.int32, (sub, t1s.shape[1]), 0)
    pieces = [t1[0] + t2s]
    for a in range(1, sub):
        pieces.append(jnp.where(rowid < K // (a + 1), t1[a] + t2s[:sub], NEG_INF))
    pieces.append(t1s[sub:] + t2[0])
    return jnp.concatenate(pieces, axis=0)


def _peer_select_kernel(x_ref, g_ref, wq_ref, keys_ref, hnt_ref, n1_ref, e1_ref, r2_ref, e2_ref):
    K = PEER_TOPK
    hf = _rmsnorm_rows(x_ref[...], g_ref[...])
    hnt_ref[...] = _pack(hf.T.astype(BF16))
    q = _dot(hf.astype(BF16), wq_ref[...]).astype(BF16)
    half = PEER_DQ // 2
    for hd in range(PEER_HEADS):
        s1 = _dot_nt(keys_ref[hd, 0], q[:, hd * PEER_DQ:hd * PEER_DQ + half])
        s2 = _dot_nt(keys_ref[hd, 1], q[:, hd * PEER_DQ + half:(hd + 1) * PEER_DQ])
        t1, t1s, rank1 = _top_values(s1, K)
        t2, t2s, rank2 = _top_values(s2, K)
        cand = _pair_candidates(t1, t1s, t2, t2s)
        tau = _kth_value(cand, K)
        z = jnp.sum(jnp.where(cand >= tau, jnp.exp(cand - (t1[0] + t2[0])), 0.0), axis=0, keepdims=True)
        rank1_b = rank1.astype(BF16)
        n1 = jnp.zeros_like(rank1_b)
        for a in range(K):
            n_a = jnp.sum(jnp.where(t1[a] + t2s >= tau, 1.0, 0.0), axis=0, keepdims=True)
            n1 = jnp.where(rank1_b == float(a), jnp.broadcast_to(n_a, s1.shape).astype(BF16), n1)
        n1_ref[hd] = n1.astype(F32)
        e1_ref[hd] = jnp.where(rank1 < float(K), jnp.exp(s1 - t1[0]), 0.0)
        r2_ref[hd] = _pack(rank2.astype(BF16))
        e2_ref[hd] = _pack((jnp.exp(s2 - t2[0]) / z).astype(BF16))


def peer_select(x, g, wq, keys, *, tt):
    T, D = x.shape
    H, NK = PEER_HEADS, PEER_NKEYS
    const2 = lambda i: (0, 0)
    row_spec = pl.BlockSpec((H, NK, tt), lambda i: (0, 0, i))
    key_spec = pl.BlockSpec((H, NK // 2, tt), lambda i: (0, 0, i))
    return pl.pallas_call(
        _peer_select_kernel,
        grid=(T // tt,),
        in_specs=[
            pl.BlockSpec((tt, D), lambda i: (i, 0)),
            pl.BlockSpec((1, D), const2),
            pl.BlockSpec(wq.shape, const2),
            pl.BlockSpec(keys.shape, lambda i: (0, 0, 0, 0)),
        ],
        out_specs=[pl.BlockSpec((D // 2, tt), lambda i: (0, i)), row_spec, row_spec, key_spec, key_spec],
        out_shape=[
            jax.ShapeDtypeStruct((D // 2, T), U32),
            jax.ShapeDtypeStruct((H, NK, T), F32),
            jax.ShapeDtypeStruct((H, NK, T), F32),
            jax.ShapeDtypeStruct((H, NK // 2, T), U32),
            jax.ShapeDtypeStruct((H, NK // 2, T), U32),
        ],
        compiler_params=_params("parallel"),
        name="peer_select",
    )(x, g, wq, keys)


def _gelu(a):
    return a * 0.5 * (1.0 + lax.erf(a * (1.0 / math.sqrt(2.0))))


MXU_DIM = 256
STEP_GROUPS = (4, 4, 4, 4)


def _peer_dense_kernel(x_ref, hnt_ref, u_ref, vt_ref, n1_ref, e1_ref, r2_ref, e2_ref, g_ref, o_ref, acc,
                       *, final):
    j = pl.program_id(1)
    tt = x_ref.shape[0]
    assert sum(STEP_GROUPS) * PEER_NKEYS == 2 * u_ref.shape[0]
    first = [sum(STEP_GROUPS[:s]) for s in range(len(STEP_GROUPS) + 1)]

    @pl.when(j == 0)
    def _():
        acc[...] = jnp.zeros_like(acc)

    hnt = _unpack(hnt_ref[...])

    def scores(s):
        rows = slice(first[s] * PEER_NKEYS // 2, first[s + 1] * PEER_NKEYS // 2)
        return _dot(_unpack(u_ref[rows, :]), hnt)

    def gates(s, tok):
        out = [None] * STEP_GROUPS[s]
        for hd in range(PEER_HEADS):
            r2 = _unpack(r2_ref[hd, :, tok])
            e2 = _unpack(e2_ref[hd, :, tok])
            for g in range(STEP_GROUPS[s]):
                r = first[s] + g
                nb = jnp.broadcast_to(n1_ref[hd, r:r + 1, tok], (PEER_NKEYS, LANES)).astype(BF16)
                eb = jnp.broadcast_to(e1_ref[hd, r:r + 1, tok], (PEER_NKEYS, LANES)).astype(BF16)
                prod = e2 * eb
                term = jnp.where(r2 < nb, prod, jnp.zeros_like(prod))
                out[g] = term if out[g] is None else out[g] + term
        return jnp.concatenate(out, axis=0)

    def weights(s, a):
        cols = []
        for t0 in range(0, tt, LANES):
            tok = slice(t0, t0 + LANES)
            cols.append(gates(s, tok) * _gelu(a[:, tok]).astype(BF16))
        return jnp.concatenate(cols, axis=1)

    a_next = scores(0)
    for s in range(len(STEP_GROUPS)):
        a_cur = a_next
        if s + 1 < len(STEP_GROUPS):
            a_next = scores(s + 1)
        w = weights(s, a_cur)
        cols = slice(first[s] * PEER_NKEYS, first[s + 1] * PEER_NKEYS)
        acc[...] += _dot(_unpack(vt_ref[:, cols]), w)

    @pl.when(j == pl.num_programs(1) - 1)
    def _():
        o = x_ref[...] + acc[...].T
        o_ref[...] = _rmsnorm_rows(o, g_ref[...]) if final else o


def peer_dense(x, hnt, u, vt, n1, e1, r2, e2, g, *, tt, ec, final):
    T, D = x.shape
    E = 2 * u.shape[0]
    H, NK = PEER_HEADS, PEER_NKEYS
    row_spec = pl.BlockSpec((H, ec // NK, tt), lambda i, j: (0, j, i))
    once = pl.Buffered(1)
    key_spec = pl.BlockSpec((H, NK // 2, tt), lambda i, j: (0, 0, i), pipeline_mode=once)
    return pl.pallas_call(
        functools.partial(_peer_dense_kernel, final=final),
        grid=(T // tt, E // ec),
        in_specs=[
            pl.BlockSpec((tt, D), lambda i, j: (i, 0), pipeline_mode=once),
            pl.BlockSpec((D // 2, tt), lambda i, j: (0, i), pipeline_mode=once),
            pl.BlockSpec((ec // 2, D), lambda i, j: (j, 0)),
            pl.BlockSpec((D // 2, ec), lambda i, j: (0, j)),
            row_spec, row_spec, key_spec, key_spec,
            pl.BlockSpec((1, D), lambda i, j: (0, 0)),
        ],
        out_specs=pl.BlockSpec((tt, D), lambda i, j: (i, 0)),
        out_shape=jax.ShapeDtypeStruct((T, D), F32),
        scratch_shapes=[pltpu.VMEM((D, tt), F32)],
        compiler_params=_params("parallel", "arbitrary"),
        name="peer_dense",
    )(x, hnt, u, vt, n1, e1, r2, e2, g)


def _pack_rows_kernel(w_ref, o_ref):
    o_ref[...] = _pack(w_ref[...].astype(BF16))


def pack_rows(w, layer, *, tr):
    _, R, C = w.shape
    return pl.pallas_call(
        _pack_rows_kernel,
        grid=(R // tr,),
        in_specs=[pl.BlockSpec((None, tr, C), lambda i: (layer, i, 0))],
        out_specs=pl.BlockSpec((tr // 2, C), lambda i: (i, 0)),
        out_shape=jax.ShapeDtypeStruct((R // 2, C), U32),
        compiler_params=_params("parallel"),
        name="pack_rows",
    )(w)


def _pack_transposed_kernel(w_ref, o_ref):
    o_ref[...] = _pack(w_ref[...].T.astype(BF16))


def pack_transposed(w, layer, *, tr):
    _, R, C = w.shape
    return pl.pallas_call(
        _pack_transposed_kernel,
        grid=(R // tr,),
        in_specs=[pl.BlockSpec((None, tr, C), lambda i: (layer, i, 0))],
        out_specs=pl.BlockSpec((C // 2, tr), lambda i: (0, i)),
        out_shape=jax.ShapeDtypeStruct((C // 2, R), U32),
        compiler_params=_params("parallel"),
        name="pack_transposed",
    )(w)


def _rope_tables(S):
    d = RET_HEAD_DIM
    inv = ROPE_BASE ** (-jnp.arange(0, d, 2, dtype=F32) / d)
    ang = jnp.arange(S, dtype=F32)[:, None] * inv[None, :]
    cos, sin = jnp.cos(ang), jnp.sin(ang)
    return jnp.concatenate([cos, cos], axis=1), jnp.concatenate([-sin, sin], axis=1)


def _tile(n, want):
    t = min(n, want)
    assert n % t == 0, (n, t)
    return t


def kernel(x, mem, mix_norm, w_in, sb_norm, ret_norm, w_out, xa_norm, mem_norm, xa_wq, xa_wkv, xa_wo,
           peer_norm, peer_wq, peer_keys, peer_u, peer_v, final_norm_g):
    B, S, D = x.shape
    M = mem.shape[1]
    T = B * S
    depth = w_in.shape[0]
    assert D == D_MODEL and S % SB_BLOCK == 0 and w_in.shape[2] == IN_COLS

    cos, sin = _rope_tables(S)
    log_gamma = jnp.log1p(-jnp.power(2.0, -5.0 - jnp.arange(RET_HEADS, dtype=F32)))
    log_gamma = jnp.broadcast_to(log_gamma[:, None, None], (RET_HEADS, 1, LANES))
    row = lambda v: v.reshape(1, -1).astype(F32)
    tm = _tile(T, 512)

    xt = x.reshape(T, D)
    mem2 = mem.reshape(B * M, D)
    for l in range(depth):
        proj = norm_matmul(xt, row(mix_norm[l]), w_in[l].astype(BF16), tm=tm, tn=IN_COLS // 7)
        proj = proj.reshape(B, S, IN_COLS)
        sb_o = sb_attention(proj, row(sb_norm[l]))
        ret_o = retention(proj, row(ret_norm[l]), cos, sin, log_gamma)
        w_o = w_out[l].astype(BF16)
        kv = norm_matmul(mem2, row(mem_norm[l]), xa_wkv[l].astype(BF16), tm=_tile(B * M, 512), tn=1024)
        xt = cross_attn(xt.reshape(B, S, D), sb_o, ret_o, w_o[:SB_WIDTH], w_o[SB_WIDTH:],
                        row(xa_norm[l]), xa_wq[l].astype(BF16), kv.reshape(B, M, 2 * D),
                        xa_wo[l].astype(BF16), tq=_tile(S, 512)).reshape(T, D)

        hn, n1, e1, r2, e2 = peer_select(xt, row(peer_norm[l]), peer_wq[l].astype(BF16),
                                         peer_keys[l].astype(BF16), tt=_tile(T, 256))
        xt = peer_dense(xt, hn, pack_rows(peer_u, l, tr=1024), pack_transposed(peer_v, l, tr=512),
                        n1, e1, r2, e2, row(final_norm_g), tt=_tile(T, 1024), ec=2048,
                        final=l == depth - 1)
    return xt.reshape(B, S, D)
```

```python
import functools
import math

import jax
import jax.numpy as jnp
from jax import lax
from jax.experimental import pallas as pl
from jax.experimental.pallas import tpu as pltpu

F32 = jnp.float32
BF16 = jnp.bfloat16
U32 = jnp.uint32

D_MODEL = 1024
SB_HEADS = 8
SB_HEAD_DIM = 64
SB_WIDTH = 512
SB_BLOCK = 128
RET_HEADS = 4
RET_HEAD_DIM = 128
RET_WIDTH = 512
RET_CHUNK = 128
ROPE_BASE = 10000.0
IN_COLS = 3 * SB_WIDTH + 4 * RET_WIDTH
XA_HEADS = 4
XA_HEAD_DIM = 256
PEER_HEADS = 8
PEER_NKEYS = 128
PEER_DQ = 256
PEER_TOPK = 16
EPS = 1e-6

LANES = 128
SUBLANES = 8
VMEM_LIMIT = 56 * 1024 * 1024
NEG_INF = float("-inf")
EXP_ZERO_BELOW = -88.0


def _params(*sem):
    return pltpu.CompilerParams(dimension_semantics=sem, vmem_limit_bytes=VMEM_LIMIT)


def _dot(a, b):
    return jnp.dot(a, b, preferred_element_type=F32)


def _dot_nt(a, b):
    return lax.dot_general(a, b, (((1,), (1,)), ((), ())), preferred_element_type=F32)


def _rmsnorm_rows(x, g):
    ms = jnp.mean(x * x, axis=-1, keepdims=True)
    return x * lax.rsqrt(ms + EPS) * g


def _unpack(words):
    return pltpu.bitcast(words, BF16)


def _pack(rows):
    return pltpu.bitcast(rows, U32)


def _norm_matmul_kernel(x_ref, g_ref, w_ref, o_ref, *, tn):
    h = _rmsnorm_rows(x_ref[...], g_ref[...]).astype(BF16)
    for n0 in range(0, w_ref.shape[1], tn):
        o_ref[:, n0:n0 + tn] = _dot(h, w_ref[:, n0:n0 + tn]).astype(o_ref.dtype)


def norm_matmul(x, g, w, *, tm, tn, out_dtype=BF16):
    T, D = x.shape
    N = w.shape[1]
    return pl.pallas_call(
        functools.partial(_norm_matmul_kernel, tn=tn),
        grid=(T // tm,),
        in_specs=[
            pl.BlockSpec((tm, D), lambda i: (i, 0)),
            pl.BlockSpec((1, D), lambda i: (0, 0)),
            pl.BlockSpec((D, N), lambda i: (0, 0)),
        ],
        out_specs=pl.BlockSpec((tm, N), lambda i: (i, 0)),
        out_shape=jax.ShapeDtypeStruct((T, N), out_dtype),
        compiler_params=_params("parallel"),
        name="norm_matmul",
    )(x, g, w)


def _softplus(z):
    return jnp.maximum(z, 0.0) + jnp.log(1.0 + jnp.exp(-jnp.abs(z)))


def _sb_kernel(q_ref, k_ref, v_ref, g_ref, o_ref, acc, run):
    blk = SB_BLOCK
    pairs = SB_WIDTH // LANES
    i = pl.program_id(1)
    scale = SB_HEAD_DIM ** -0.5
    first = lax.broadcasted_iota(jnp.int32, (1, LANES), 1) < SB_HEAD_DIM
    row = lax.broadcasted_iota(jnp.int32, (2 * blk, blk), 0) % blk
    col = lax.broadcasted_iota(jnp.int32, (2 * blk, blk), 1)
    causal = col < row
    r2 = lax.broadcasted_iota(jnp.int32, (2 * blk, 2 * blk), 0) % blk
    c2 = lax.broadcasted_iota(jnp.int32, (2 * blk, 2 * blk), 1)
    suffix = jnp.where((c2 >= blk) | (r2 > c2), 1.0, 0.0).astype(BF16)

    def split_heads(t):
        zero = jnp.zeros_like(t)
        return jnp.concatenate([jnp.where(first, t, zero), jnp.where(first, zero, t)], axis=0)

    chains = [(rb, slice(p * LANES, (p + 1) * LANES)) for rb in range(q_ref.shape[0]) for p in range(pairs)]

    def add_key_block(j, diagonal):
        rows = pl.ds(pl.multiple_of(j * blk, blk), blk)
        zs = [_dot_nt(split_heads(q_ref[rb, :, c]), k_ref[rb, rows, c]) * scale for rb, c in chains]
        sps = [_softplus(z) for z in zs]
        sums = []
        for sp in sps:
            lnb = jnp.where(causal, -sp, 0.0) if diagonal else -sp
            hi = lnb.astype(BF16)
            lo = (lnb - hi.astype(F32)).astype(BF16)
            sums.append(_dot(jnp.concatenate([hi, lo], axis=1), suffix))
        outs = []
        for n, (rb, c) in enumerate(chains):
            stick, tot = sums[n][:, :blk], sums[n][:, blk:]
            if diagonal:
                a = jnp.where(causal, jnp.exp(zs[n] - sps[n] + stick), 0.0)
                run[n] = tot
            else:
                a = jnp.exp(zs[n] - sps[n] + stick + run[n])
                run[n] += tot
            a = a.astype(BF16)
            outs.append(_dot(jnp.concatenate([a[:blk], a[blk:]], axis=1),
                             split_heads(v_ref[rb, rows, c])))
        for n in range(len(chains)):
            acc[n] = outs[n] if diagonal else acc[n] + outs[n]

    def highest_run():
        m = run[0]
        for n in range(1, len(chains)):
            m = jnp.maximum(m, run[n])
        return jnp.max(m)

    add_key_block(i, True)

    def cond(carry):
        jj, top = carry
        return (jj <= i) & (top > EXP_ZERO_BELOW)

    def body(carry):
        jj, _ = carry
        add_key_block(i - jj, False)
        return jj + 1, highest_run()

    lax.while_loop(cond, body, (jnp.int32(1), highest_run()))

    for n, (rb, cols) in enumerate(chains):
        o = acc[n]
        sq = o * o
        ms0 = jnp.sum(jnp.where(first, sq, 0.0), axis=-1, keepdims=True)
        ms1 = jnp.sum(jnp.where(first, 0.0, sq), axis=-1, keepdims=True)
        ms = jnp.where(first, ms0, ms1) * (1.0 / SB_HEAD_DIM)
        o_ref[rb, :, cols] = (o * lax.rsqrt(ms + EPS) * g_ref[:, cols]).astype(o_ref.dtype)


def sb_attention(proj, sb_norm, *, rows):
    B, S, _ = proj.shape
    chains = rows * SB_WIDTH // LANES
    return pl.pallas_call(
        _sb_kernel,
        grid=(B // rows, S // SB_BLOCK),
        in_specs=[
            pl.BlockSpec((rows, SB_BLOCK, SB_WIDTH), lambda b, i: (b, i, 0)),
            pl.BlockSpec((rows, S, SB_WIDTH), lambda b, i: (b, 0, 1)),
            pl.BlockSpec((rows, S, SB_WIDTH), lambda b, i: (b, 0, 2)),
            pl.BlockSpec((1, SB_WIDTH), lambda b, i: (0, 0)),
        ],
        out_specs=pl.BlockSpec((rows, SB_BLOCK, SB_WIDTH), lambda b, i: (b, i, 0)),
        out_shape=jax.ShapeDtypeStruct((B, S, SB_WIDTH), BF16),
        scratch_shapes=[pltpu.VMEM((chains, SB_BLOCK, LANES), F32),
                        pltpu.VMEM((chains, 2 * SB_BLOCK, LANES), F32)],
        compiler_params=_params("parallel", "arbitrary"),
        name="sb_attention",
    )(proj, proj, proj, sb_norm)


def _retention_kernel(q_ref, k_ref, v_ref, gate_ref, cos_ref, sin_ref, lg_ref, g_ref, o_ref, state):
    C = RET_CHUNK
    n = pl.program_id(1)

    @pl.when(n == 0)
    def _():
        state[...] = jnp.zeros_like(state)

    cos, sin = cos_ref[...], sin_ref[...]

    def rope(t):
        t = t.astype(F32)
        return t * cos + pltpu.roll(t, RET_HEAD_DIM // 2, 1) * sin

    row = lax.broadcasted_iota(jnp.int32, (C, C), 0).astype(F32)
    col = lax.broadcasted_iota(jnp.int32, (C, C), 1).astype(F32)
    diff = row - col
    heads = range(RET_HEADS)
    col_of = [slice(hd * RET_HEAD_DIM, (hd + 1) * RET_HEAD_DIM) for hd in heads]
    lgs = [lg_ref[hd] for hd in heads]
    qcs = [rope(q_ref[:, c]) for c in col_of]
    kcs = [rope(k_ref[:, c]) * RET_HEAD_DIM ** -0.5 for c in col_of]
    inner = [_dot_nt(qcs[hd].astype(BF16), kcs[hd].astype(BF16))
             * jnp.where(diff >= 0, jnp.exp(diff * lgs[hd]), 0.0) for hd in heads]
    sts = [state[hd] for hd in heads]
    outs = [_dot(inner[hd].astype(BF16), v_ref[:, col_of[hd]])
            + _dot((qcs[hd] * jnp.exp((row + 1) * lgs[hd])).astype(BF16), sts[hd].astype(BF16))
            for hd in heads]
    for hd in heads:
        zeta = jnp.exp((C - 1 - row) * lgs[hd])
        state[hd] = jnp.exp(C * lgs[hd]) * sts[hd] + _dot((kcs[hd] * zeta).T.astype(BF16),
                                                          v_ref[:, col_of[hd]])
    for hd in heads:
        o = outs[hd]
        mu = jnp.mean(o, axis=-1, keepdims=True)
        var = jnp.mean((o - mu) ** 2, axis=-1, keepdims=True)
        y = (o - mu) * lax.rsqrt(var + EPS) * g_ref[:, col_of[hd]]
        gate = gate_ref[:, col_of[hd]].astype(F32)
        o_ref[:, col_of[hd]] = (y * (gate * jax.nn.sigmoid(gate))).astype(o_ref.dtype)


def retention(proj, ret_norm, cos, sin, log_gamma):
    B, S, _ = proj.shape
    C = RET_CHUNK
    base = 3 * SB_WIDTH // RET_WIDTH

    def col(group):
        return pl.BlockSpec((None, C, RET_WIDTH), lambda b, n: (b, n, base + group))

    return pl.pallas_call(
        _retention_kernel,
        grid=(B, S // C),
        in_specs=[
            col(0), col(1), col(2), col(3),
            pl.BlockSpec((C, LANES), lambda b, n: (n, 0)),
            pl.BlockSpec((C, LANES), lambda b, n: (n, 0)),
            pl.BlockSpec((RET_HEADS, 1, LANES), lambda b, n: (0, 0, 0)),
            pl.BlockSpec((1, RET_WIDTH), lambda b, n: (0, 0)),
        ],
        out_specs=pl.BlockSpec((None, C, RET_WIDTH), lambda b, n: (b, n, 0)),
        out_shape=jax.ShapeDtypeStruct((B, S, RET_WIDTH), BF16),
        scratch_shapes=[pltpu.VMEM((RET_HEADS, RET_HEAD_DIM, RET_HEAD_DIM), F32)],
        compiler_params=_params("parallel", "arbitrary"),
        name="retention",
    )(proj, proj, proj, proj, cos, sin, log_gamma, ret_norm)


def _cross_attn_kernel(x_ref, a_ref, b_ref, wa_ref, wb_ref, g_ref, wq_ref, kv_ref, wo_ref, o_ref):
    x = x_ref[...] + _dot(a_ref[...], wa_ref[...]) + _dot(b_ref[...], wb_ref[...])
    h = _rmsnorm_rows(x, g_ref[...]).astype(BF16)
    q = _dot(h, wq_ref[...]).astype(BF16)
    outs = []
    for hd in range(XA_HEADS):
        lo = hd * XA_HEAD_DIM
        k = kv_ref[:, lo:lo + XA_HEAD_DIM]
        v = kv_ref[:, D_MODEL + lo:D_MODEL + lo + XA_HEAD_DIM]
        s = _dot_nt(q[:, lo:lo + XA_HEAD_DIM], k) * XA_HEAD_DIM ** -0.5
        e = jnp.exp(s - jnp.max(s, axis=-1, keepdims=True))
        p = e / jnp.sum(e, axis=-1, keepdims=True)
        outs.append(_dot(p.astype(BF16), v).astype(BF16))
    o_ref[...] = x + _dot(jnp.concatenate(outs, axis=1), wo_ref[...])


def cross_attn(x, a, b, wa, wb, g, wq, kv, wo, *, tq):
    B, S, D = x.shape
    M = kv.shape[1]
    const = lambda b, i: (0, 0)
    tile = lambda w: pl.BlockSpec((None, tq, w), lambda b, i: (b, i, 0))
    return pl.pallas_call(
        _cross_attn_kernel,
        grid=(B, S // tq),
        in_specs=[
            tile(D), tile(a.shape[2]), tile(b.shape[2]),
            pl.BlockSpec(wa.shape, const),
            pl.BlockSpec(wb.shape, const),
            pl.BlockSpec((1, D), const),
            pl.BlockSpec((D, D), const),
            pl.BlockSpec((None, M, 2 * D), lambda b, i: (b, 0, 0)),
            pl.BlockSpec((D, D), const),
        ],
        out_specs=tile(D),
        out_shape=jax.ShapeDtypeStruct((B, S, D), F32),
        compiler_params=_params("parallel", "arbitrary"),
        name="cross_attn",
    )(x, a, b, wa, wb, g, wq, kv, wo)


def _kth_value(s, k):
    for _ in range(k - 1):
        s = jnp.where(s == jnp.max(s, axis=0, keepdims=True), NEG_INF, s)
    return jnp.max(s, axis=0, keepdims=True)


INT32_MIN = -2 ** 31
INT32_MAX = 2 ** 31 - 1


def _order_key(v):
    b = pltpu.bitcast(v, jnp.int32) if v.dtype == F32 else v
    k = b ^ ((b >> 31) & jnp.int32(INT32_MAX))
    return k if v.dtype == F32 else pltpu.bitcast(k, F32)


def _top_values(s, k):
    rowid = lax.broadcasted_iota(jnp.int32, (k, s.shape[1]), 0)
    key = _order_key(s)
    vals, stacked = [], jnp.full((k, s.shape[1]), INT32_MIN, jnp.int32)
    for r in range(k):
        m = jnp.max(key, axis=0, keepdims=True)
        vals.append(_order_key(m))
        stacked = jnp.where(rowid == r, m, stacked)
        key = jnp.where(key == m, jnp.int32(INT32_MIN + r), key)
    rank = jnp.where(key < jnp.int32(INT32_MIN + k), key & jnp.int32(INT32_MAX), k).astype(F32)
    return vals, _order_key(stacked), rank


def _pair_candidates(t1, t1s, t2, t2s):
    K, sub = PEER_TOPK, 8
    rowid = lax.broadcasted_iota(jnp.int32, (sub, t1s.shape[1]), 0)
    pieces = [t1[0] + t2s]
    for a in range(1, sub):
        pieces.append(jnp.where(rowid < K // (a + 1), t1[a] + t2s[:sub], NEG_INF))
    pieces.append(t1s[sub:] + t2[0])
    return jnp.concatenate(pieces, axis=0)


def _peer_select_kernel(x_ref, g_ref, wq_ref, keys_ref, hnt_ref, n1_ref, e1_ref, r2_ref, e2_ref):
    K = PEER_TOPK
    hf = _rmsnorm_rows(x_ref[...], g_ref[...])
    hnt_ref[...] = _pack(hf.T.astype(BF16))
    q = _dot(hf.astype(BF16), wq_ref[...]).astype(BF16)
    half = PEER_DQ // 2
    for hd in range(PEER_HEADS):
        s1 = _dot_nt(keys_ref[hd, 0], q[:, hd * PEER_DQ:hd * PEER_DQ + half])
        s2 = _dot_nt(keys_ref[hd, 1], q[:, hd * PEER_DQ + half:(hd + 1) * PEER_DQ])
        t1, t1s, rank1 = _top_values(s1, K)
        t2, t2s, rank2 = _top_values(s2, K)
        cand = _pair_candidates(t1, t1s, t2, t2s)
        tau = _kth_value(cand, K)
        z = jnp.sum(jnp.where(cand >= tau, jnp.exp(cand - (t1[0] + t2[0])), 0.0), axis=0, keepdims=True)
        rank1_b = rank1.astype(BF16)
        n1 = jnp.zeros_like(rank1_b)
        for a in range(K):
            n_a = jnp.sum(jnp.where(t1[a] + t2s >= tau, 1.0, 0.0), axis=0, keepdims=True)
            n1 = jnp.where(rank1_b == float(a), jnp.broadcast_to(n_a, s1.shape).astype(BF16), n1)
        n1_ref[hd] = n1.astype(F32)
        e1_ref[hd] = jnp.where(rank1 < float(K), jnp.exp(s1 - t1[0]), 0.0)
        r2_ref[hd] = _pack(rank2.astype(BF16))
        e2_ref[hd] = _pack((jnp.exp(s2 - t2[0]) / z).astype(BF16))


def peer_select(x, g, wq, keys, *, tt):
    T, D = x.shape
    H, NK = PEER_HEADS, PEER_NKEYS
    const2 = lambda i: (0, 0)
    row_spec = pl.BlockSpec((H, NK, tt), lambda i: (0, 0, i))
    key_spec = pl.BlockSpec((H, NK // 2, tt), lambda i: (0, 0, i))
    return pl.pallas_call(
        _peer_select_kernel,
        grid=(T // tt,),
        in_specs=[
            pl.BlockSpec((tt, D), lambda i: (i, 0)),
            pl.BlockSpec((1, D), const2),
            pl.BlockSpec(wq.shape, const2),
            pl.BlockSpec(keys.shape, lambda i: (0, 0, 0, 0)),
        ],
        out_specs=[pl.BlockSpec((D // 2, tt), lambda i: (0, i)), row_spec, row_spec, key_spec, key_spec],
        out_shape=[
            jax.ShapeDtypeStruct((D // 2, T), U32),
            jax.ShapeDtypeStruct((H, NK, T), F32),
            jax.ShapeDtypeStruct((H, NK, T), F32),
            jax.ShapeDtypeStruct((H, NK // 2, T), U32),
            jax.ShapeDtypeStruct((H, NK // 2, T), U32),
        ],
        compiler_params=_params("parallel"),
        name="peer_select",
    )(x, g, wq, keys)


def _gelu(a):
    return a * 0.5 * (1.0 + lax.erf(a * (1.0 / math.sqrt(2.0))))


MXU_DIM = 256
STEP_GROUPS = (4, 4)


def _peer_dense_kernel(x_ref, hnt_ref, u_ref, vt_ref, n1_ref, e1_ref, r2_ref, e2_ref, g_ref, o_ref, acc,
                       *, final):
    j = pl.program_id(1)
    tt = x_ref.shape[0]
    assert sum(STEP_GROUPS) * PEER_NKEYS == 2 * u_ref.shape[0]
    first = [sum(STEP_GROUPS[:s]) for s in range(len(STEP_GROUPS) + 1)]

    @pl.when(j == 0)
    def _():
        acc[...] = jnp.zeros_like(acc)

    hnt = _unpack(hnt_ref[...])

    def scores(s):
        rows = slice(first[s] * PEER_NKEYS // 2, first[s + 1] * PEER_NKEYS // 2)
        return _dot(_unpack(u_ref[rows, :]), hnt)

    def gates(s, tok):
        out = [None] * STEP_GROUPS[s]
        for hd in range(PEER_HEADS):
            r2 = _unpack(r2_ref[hd, :, tok])
            e2 = _unpack(e2_ref[hd, :, tok])
            for g in range(STEP_GROUPS[s]):
                r = first[s] + g
                nb = jnp.broadcast_to(n1_ref[hd, r:r + 1, tok], (PEER_NKEYS, LANES)).astype(BF16)
                eb = jnp.broadcast_to(e1_ref[hd, r:r + 1, tok], (PEER_NKEYS, LANES)).astype(BF16)
                prod = e2 * eb
                term = jnp.where(r2 < nb, prod, jnp.zeros_like(prod))
                out[g] = term if out[g] is None else out[g] + term
        return jnp.concatenate(out, axis=0)

    def weights(s, a):
        cols = []
        for t0 in range(0, tt, LANES):
            tok = slice(t0, t0 + LANES)
            cols.append(gates(s, tok) * _gelu(a[:, tok]).astype(BF16))
        return jnp.concatenate(cols, axis=1)

    a_next = scores(0)
    for s in range(len(STEP_GROUPS)):
        a_cur = a_next
        if s + 1 < len(STEP_GROUPS):
            a_next = scores(s + 1)
        w = weights(s, a_cur)
        cols = slice(first[s] * PEER_NKEYS, first[s + 1] * PEER_NKEYS)
        acc[...] += _dot(_unpack(vt_ref[:, cols]), w)

    @pl.when(j == pl.num_programs(1) - 1)
    def _():
        o = x_ref[...] + acc[...].T
        o_ref[...] = _rmsnorm_rows(o, g_ref[...]) if final else o


def peer_dense(x, hnt, u, vt, n1, e1, r2, e2, g, *, tt, ec, final):
    T, D = x.shape
    E = 2 * u.shape[0]
    H, NK = PEER_HEADS, PEER_NKEYS
    row_spec = pl.BlockSpec((H, ec // NK, tt), lambda i, j: (0, j, i))
    key_spec = pl.BlockSpec((H, NK // 2, tt), lambda i, j: (0, 0, i))
    return pl.pallas_call(
        functools.partial(_peer_dense_kernel, final=final),
        grid=(T // tt, E // ec),
        in_specs=[
            pl.BlockSpec((tt, D), lambda i, j: (i, 0)),
            pl.BlockSpec((D // 2, tt), lambda i, j: (0, i)),
            pl.BlockSpec((ec // 2, D), lambda i, j: (j, 0)),
            pl.BlockSpec((D // 2, ec), lambda i, j: (0, j)),
            row_spec, row_spec, key_spec, key_spec,
            pl.BlockSpec((1, D), lambda i, j: (0, 0)),
        ],
        out_specs=pl.BlockSpec((tt, D), lambda i, j: (i, 0)),
        out_shape=jax.ShapeDtypeStruct((T, D), F32),
        scratch_shapes=[pltpu.VMEM((D, tt), F32)],
        compiler_params=_params("parallel", "arbitrary"),
        name="peer_dense",
    )(x, hnt, u, vt, n1, e1, r2, e2, g)


def _pack_rows_kernel(w_ref, o_ref):
    o_ref[...] = _pack(w_ref[...].astype(BF16))


def pack_rows(w, layer, *, tr):
    _, R, C = w.shape
    return pl.pallas_call(
        _pack_rows_kernel,
        grid=(R // tr,),
        in_specs=[pl.BlockSpec((None, tr, C), lambda i: (layer, i, 0))],
        out_specs=pl.BlockSpec((tr // 2, C), lambda i: (i, 0)),
        out_shape=jax.ShapeDtypeStruct((R // 2, C), U32),
        compiler_params=_params("parallel"),
        name="pack_rows",
    )(w)


def _pack_transposed_kernel(w_ref, o_ref):
    o_ref[...] = _pack(w_ref[...].T.astype(BF16))


def pack_transposed(w, layer, *, tr):
    _, R, C = w.shape
    return pl.pallas_call(
        _pack_transposed_kernel,
        grid=(R // tr,),
        in_specs=[pl.BlockSpec((None, tr, C), lambda i: (layer, i, 0))],
        out_specs=pl.BlockSpec((C // 2, tr), lambda i: (0, i)),
        out_shape=jax.ShapeDtypeStruct((C // 2, R), U32),
        compiler_params=_params("parallel"),
        name="pack_transposed",
    )(w)


def _rope_tables(S):
    d = RET_HEAD_DIM
    inv = ROPE_BASE ** (-jnp.arange(0, d, 2, dtype=F32) / d)
    ang = jnp.arange(S, dtype=F32)[:, None] * inv[None, :]
    cos, sin = jnp.cos(ang), jnp.sin(ang)
    return jnp.concatenate([cos, cos], axis=1), jnp.concatenate([-sin, sin], axis=1)


def _tile(n, want):
    t = min(n, want)
    assert n % t == 0, (n, t)
    return t


def kernel(x, mem, mix_norm, w_in, sb_norm, ret_norm, w_out, xa_norm, mem_norm, xa_wq, xa_wkv, xa_wo,
           peer_norm, peer_wq, peer_keys, peer_u, peer_v, final_norm_g):
    B, S, D = x.shape
    M = mem.shape[1]
    T = B * S
    depth = w_in.shape[0]
    assert D == D_MODEL and S % SB_BLOCK == 0 and w_in.shape[2] == IN_COLS

    cos, sin = _rope_tables(S)
    log_gamma = jnp.log1p(-jnp.power(2.0, -5.0 - jnp.arange(RET_HEADS, dtype=F32)))
    log_gamma = jnp.broadcast_to(log_gamma[:, None, None], (RET_HEADS, 1, LANES))
    row = lambda v: v.reshape(1, -1).astype(F32)
    tm = _tile(T, 512)

    xt = x.reshape(T, D)
    mem2 = mem.reshape(B * M, D)
    for l in range(depth):
        proj = norm_matmul(xt, row(mix_norm[l]), w_in[l].astype(BF16), tm=tm, tn=IN_COLS // 7)
        proj = proj.reshape(B, S, IN_COLS)
        sb_o = sb_attention(proj, row(sb_norm[l]), rows=2 if B % 2 == 0 else 1)
        ret_o = retention(proj, row(ret_norm[l]), cos, sin, log_gamma)
        w_o = w_out[l].astype(BF16)
        kv = norm_matmul(mem2, row(mem_norm[l]), xa_wkv[l].astype(BF16), tm=_tile(B * M, 512), tn=1024)
        xt = cross_attn(xt.reshape(B, S, D), sb_o, ret_o, w_o[:SB_WIDTH], w_o[SB_WIDTH:],
                        row(xa_norm[l]), xa_wq[l].astype(BF16), kv.reshape(B, M, 2 * D),
                        xa_wo[l].astype(BF16), tq=_tile(S, 512)).reshape(T, D)

        hn, n1, e1, r2, e2 = peer_select(xt, row(peer_norm[l]), peer_wq[l].astype(BF16),
                                         peer_keys[l].astype(BF16), tt=_tile(T, 256))
        xt = peer_dense(xt, hn, pack_rows(peer_u, l, tr=1024), pack_transposed(peer_v, l, tr=512),
                        n1, e1, r2, e2, row(final_norm_g), tt=_tile(T, 1024), ec=1024,
                        final=l == depth - 1)
    return xt.reshape(B, S, D)
```

```python
import functools
import math

import jax
import jax.numpy as jnp
from jax import lax
from jax.experimental import pallas as pl
from jax.experimental.pallas import tpu as pltpu

F32 = jnp.float32
BF16 = jnp.bfloat16
U32 = jnp.uint32

D_MODEL = 1024
SB_HEADS = 8
SB_HEAD_DIM = 64
SB_WIDTH = 512
SB_BLOCK = 128
RET_HEADS = 4
RET_HEAD_DIM = 128
RET_WIDTH = 512
RET_CHUNK = 128
ROPE_BASE = 10000.0
IN_COLS = 3 * SB_WIDTH + 4 * RET_WIDTH
XA_HEADS = 4
XA_HEAD_DIM = 256
PEER_HEADS = 8
PEER_NKEYS = 128
PEER_DQ = 256
PEER_TOPK = 16
EPS = 1e-6

LANES = 128
SUBLANES = 8
VMEM_LIMIT = 56 * 1024 * 1024
NEG_INF = float("-inf")
EXP_ZERO_BELOW = -88.0


def _params(*sem):
    return pltpu.CompilerParams(dimension_semantics=sem, vmem_limit_bytes=VMEM_LIMIT)


def _dot(a, b):
    return jnp.dot(a, b, preferred_element_type=F32)


def _dot_nt(a, b):
    return lax.dot_general(a, b, (((1,), (1,)), ((), ())), preferred_element_type=F32)


def _rmsnorm_rows(x, g):
    ms = jnp.mean(x * x, axis=-1, keepdims=True)
    return x * lax.rsqrt(ms + EPS) * g


def _unpack(words):
    return pltpu.bitcast(words, BF16)


def _pack(rows):
    return pltpu.bitcast(rows, U32)


def _norm_matmul_kernel(x_ref, g_ref, w_ref, o_ref, *, tn):
    h = _rmsnorm_rows(x_ref[...], g_ref[...]).astype(BF16)
    for n0 in range(0, w_ref.shape[1], tn):
        o_ref[:, n0:n0 + tn] = _dot(h, w_ref[:, n0:n0 + tn]).astype(o_ref.dtype)


def norm_matmul(x, g, w, *, tm, tn, out_dtype=BF16):
    T, D = x.shape
    N = w.shape[1]
    return pl.pallas_call(
        functools.partial(_norm_matmul_kernel, tn=tn),
        grid=(T // tm,),
        in_specs=[
            pl.BlockSpec((tm, D), lambda i: (i, 0)),
            pl.BlockSpec((1, D), lambda i: (0, 0)),
            pl.BlockSpec((D, N), lambda i: (0, 0)),
        ],
        out_specs=pl.BlockSpec((tm, N), lambda i: (i, 0)),
        out_shape=jax.ShapeDtypeStruct((T, N), out_dtype),
        compiler_params=_params("parallel"),
        name="norm_matmul",
    )(x, g, w)


def _softplus(z):
    return jnp.maximum(z, 0.0) + jnp.log(1.0 + jnp.exp(-jnp.abs(z)))


def _sb_kernel(q_ref, k_ref, v_ref, g_ref, o_ref, acc, run):
    blk = SB_BLOCK
    pairs = SB_WIDTH // LANES
    i = pl.program_id(1)
    scale = SB_HEAD_DIM ** -0.5
    first = lax.broadcasted_iota(jnp.int32, (1, LANES), 1) < SB_HEAD_DIM
    row = lax.broadcasted_iota(jnp.int32, (2 * blk, blk), 0) % blk
    col = lax.broadcasted_iota(jnp.int32, (2 * blk, blk), 1)
    causal = col < row
    r2 = lax.broadcasted_iota(jnp.int32, (2 * blk, 2 * blk), 0) % blk
    c2 = lax.broadcasted_iota(jnp.int32, (2 * blk, 2 * blk), 1)
    suffix = jnp.where((c2 >= blk) | (r2 > c2), 1.0, 0.0).astype(BF16)

    def split_heads(t):
        zero = jnp.zeros_like(t)
        return jnp.concatenate([jnp.where(first, t, zero), jnp.where(first, zero, t)], axis=0)

    chains = [(rb, slice(p * LANES, (p + 1) * LANES)) for rb in range(q_ref.shape[0]) for p in range(pairs)]

    def add_key_block(j, diagonal):
        rows = pl.ds(pl.multiple_of(j * blk, blk), blk)
        zs = [_dot_nt(split_heads(q_ref[rb, :, c]), k_ref[rb, rows, c]) * scale for rb, c in chains]
        sps = [_softplus(z) for z in zs]
        sums = []
        for sp in sps:
            lnb = jnp.where(causal, -sp, 0.0) if diagonal else -sp
            hi = lnb.astype(BF16)
            lo = (lnb - hi.astype(F32)).astype(BF16)
            sums.append(_dot(jnp.concatenate([hi, lo], axis=1), suffix))
        outs = []
        for n, (rb, c) in enumerate(chains):
            stick, tot = sums[n][:, :blk], sums[n][:, blk:]
            if diagonal:
                a = jnp.where(causal, jnp.exp(zs[n] - sps[n] + stick), 0.0)
                run[n] = tot
            else:
                a = jnp.exp(zs[n] - sps[n] + stick + run[n])
                run[n] += tot
            a = a.astype(BF16)
            outs.append(_dot(jnp.concatenate([a[:blk], a[blk:]], axis=1),
                             split_heads(v_ref[rb, rows, c])))
        for n in range(len(chains)):
            acc[n] = outs[n] if diagonal else acc[n] + outs[n]

    def highest_run():
        m = run[0]
        for n in range(1, len(chains)):
            m = jnp.maximum(m, run[n])
        return jnp.max(m)

    add_key_block(i, True)

    def cond(carry):
        jj, top = carry
        return (jj <= i) & (top > EXP_ZERO_BELOW)

    def body(carry):
        jj, _ = carry
        add_key_block(i - jj, False)
        return jj + 1, highest_run()

    lax.while_loop(cond, body, (jnp.int32(1), highest_run()))

    for n, (rb, cols) in enumerate(chains):
        o = acc[n]
        sq = o * o
        ms0 = jnp.sum(jnp.where(first, sq, 0.0), axis=-1, keepdims=True)
        ms1 = jnp.sum(jnp.where(first, 0.0, sq), axis=-1, keepdims=True)
        ms = jnp.where(first, ms0, ms1) * (1.0 / SB_HEAD_DIM)
        o_ref[rb, :, cols] = (o * lax.rsqrt(ms + EPS) * g_ref[:, cols]).astype(o_ref.dtype)


def sb_attention(proj, sb_norm, *, rows):
    B, S, _ = proj.shape
    chains = rows * SB_WIDTH // LANES
    return pl.pallas_call(
        _sb_kernel,
        grid=(B // rows, S // SB_BLOCK),
        in_specs=[
            pl.BlockSpec((rows, SB_BLOCK, SB_WIDTH), lambda b, i: (b, i, 0)),
            pl.BlockSpec((rows, S, SB_WIDTH), lambda b, i: (b, 0, 1)),
            pl.BlockSpec((rows, S, SB_WIDTH), lambda b, i: (b, 0, 2)),
            pl.BlockSpec((1, SB_WIDTH), lambda b, i: (0, 0)),
        ],
        out_specs=pl.BlockSpec((rows, SB_BLOCK, SB_WIDTH), lambda b, i: (b, i, 0)),
        out_shape=jax.ShapeDtypeStruct((B, S, SB_WIDTH), BF16),
        scratch_shapes=[pltpu.VMEM((chains, SB_BLOCK, LANES), F32),
                        pltpu.VMEM((chains, 2 * SB_BLOCK, LANES), F32)],
        compiler_params=_params("parallel", "arbitrary"),
        name="sb_attention",
    )(proj, proj, proj, sb_norm)


def _retention_kernel(q_ref, k_ref, v_ref, gate_ref, cos_ref, sin_ref, lg_ref, g_ref, o_ref, state):
    C = RET_CHUNK
    n = pl.program_id(1)

    @pl.when(n == 0)
    def _():
        state[...] = jnp.zeros_like(state)

    cos, sin = cos_ref[...], sin_ref[...]

    def rope(t):
        t = t.astype(F32)
        return t * cos + pltpu.roll(t, RET_HEAD_DIM // 2, 1) * sin

    row = lax.broadcasted_iota(jnp.int32, (C, C), 0).astype(F32)
    col = lax.broadcasted_iota(jnp.int32, (C, C), 1).astype(F32)
    diff = row - col
    heads = range(RET_HEADS)
    col_of = [slice(hd * RET_HEAD_DIM, (hd + 1) * RET_HEAD_DIM) for hd in heads]
    lgs = [lg_ref[hd] for hd in heads]
    qcs = [rope(q_ref[:, c]) for c in col_of]
    kcs = [rope(k_ref[:, c]) * RET_HEAD_DIM ** -0.5 for c in col_of]
    inner = [_dot_nt(qcs[hd].astype(BF16), kcs[hd].astype(BF16))
             * jnp.where(diff >= 0, jnp.exp(diff * lgs[hd]), 0.0) for hd in heads]
    sts = [state[hd] for hd in heads]
    outs = [_dot(inner[hd].astype(BF16), v_ref[:, col_of[hd]])
            + _dot((qcs[hd] * jnp.exp((row + 1) * lgs[hd])).astype(BF16), sts[hd].astype(BF16))
            for hd in heads]
    for hd in heads:
        zeta = jnp.exp((C - 1 - row) * lgs[hd])
        state[hd] = jnp.exp(C * lgs[hd]) * sts[hd] + _dot((kcs[hd] * zeta).T.astype(BF16),
                                                          v_ref[:, col_of[hd]])
    for hd in heads:
        o = outs[hd]
        mu = jnp.mean(o, axis=-1, keepdims=True)
        var = jnp.mean((o - mu) ** 2, axis=-1, keepdims=True)
        y = (o - mu) * lax.rsqrt(var + EPS) * g_ref[:, col_of[hd]]
        gate = gate_ref[:, col_of[hd]].astype(F32)
        o_ref[:, col_of[hd]] = (y * (gate * jax.nn.sigmoid(gate))).astype(o_ref.dtype)


def retention(proj, ret_norm, cos, sin, log_gamma):
    B, S, _ = proj.shape
    C = RET_CHUNK
    base = 3 * SB_WIDTH // RET_WIDTH

    def col(group):
        return pl.BlockSpec((None, C, RET_WIDTH), lambda b, n: (b, n, base + group))

    return pl.pallas_call(
        _retention_kernel,
        grid=(B, S // C),
        in_specs=[
            col(0), col(1), col(2), col(3),
            pl.BlockSpec((C, LANES), lambda b, n: (n, 0)),
            pl.BlockSpec((C, LANES), lambda b, n: (n, 0)),
            pl.BlockSpec((RET_HEADS, 1, LANES), lambda b, n: (0, 0, 0)),
            pl.BlockSpec((1, RET_WIDTH), lambda b, n: (0, 0)),
        ],
        out_specs=pl.BlockSpec((None, C, RET_WIDTH), lambda b, n: (b, n, 0)),
        out_shape=jax.ShapeDtypeStruct((B, S, RET_WIDTH), BF16),
        scratch_shapes=[pltpu.VMEM((RET_HEADS, RET_HEAD_DIM, RET_HEAD_DIM), F32)],
        compiler_params=_params("parallel", "arbitrary"),
        name="retention",
    )(proj, proj, proj, proj, cos, sin, log_gamma, ret_norm)


def _cross_attn_kernel(x_ref, a_ref, b_ref, wa_ref, wb_ref, g_ref, wq_ref, kv_ref, wo_ref, o_ref):
    x = x_ref[...] + _dot(a_ref[...], wa_ref[...]) + _dot(b_ref[...], wb_ref[...])
    h = _rmsnorm_rows(x, g_ref[...]).astype(BF16)
    q = _dot(h, wq_ref[...]).astype(BF16)
    outs = []
    for hd in range(XA_HEADS):
        lo = hd * XA_HEAD_DIM
        k = kv_ref[:, lo:lo + XA_HEAD_DIM]
        v = kv_ref[:, D_MODEL + lo:D_MODEL + lo + XA_HEAD_DIM]
        s = _dot_nt(q[:, lo:lo + XA_HEAD_DIM], k) * XA_HEAD_DIM ** -0.5
        e = jnp.exp(s - jnp.max(s, axis=-1, keepdims=True))
        p = e / jnp.sum(e, axis=-1, keepdims=True)
        outs.append(_dot(p.astype(BF16), v).astype(BF16))
    o_ref[...] = x + _dot(jnp.concatenate(outs, axis=1), wo_ref[...])


def cross_attn(x, a, b, wa, wb, g, wq, kv, wo, *, tq):
    B, S, D = x.shape
    M = kv.shape[1]
    const = lambda b, i: (0, 0)
    tile = lambda w: pl.BlockSpec((None, tq, w), lambda b, i: (b, i, 0))
    return pl.pallas_call(
        _cross_attn_kernel,
        grid=(B, S // tq),
        in_specs=[
            tile(D), tile(a.shape[2]), tile(b.shape[2]),
            pl.BlockSpec(wa.shape, const),
            pl.BlockSpec(wb.shape, const),
            pl.BlockSpec((1, D), const),
            pl.BlockSpec((D, D), const),
            pl.BlockSpec((None, M, 2 * D), lambda b, i: (b, 0, 0)),
            pl.BlockSpec((D, D), const),
        ],
        out_specs=tile(D),
        out_shape=jax.ShapeDtypeStruct((B, S, D), F32),
        compiler_params=_params("parallel", "arbitrary"),
        name="cross_attn",
    )(x, a, b, wa, wb, g, wq, kv, wo)


def _kth_value(s, k):
    for _ in range(k - 1):
        s = jnp.where(s == jnp.max(s, axis=0, keepdims=True), NEG_INF, s)
    return jnp.max(s, axis=0, keepdims=True)


INT32_MIN = -2 ** 31
INT32_MAX = 2 ** 31 - 1


def _order_key(v):
    b = pltpu.bitcast(v, jnp.int32) if v.dtype == F32 else v
    k = b ^ ((b >> 31) & jnp.int32(INT32_MAX))
    return k if v.dtype == F32 else pltpu.bitcast(k, F32)


def _top_values(s, k):
    rowid = lax.broadcasted_iota(jnp.int32, (k, s.shape[1]), 0)
    key = _order_key(s)
    vals, stacked = [], jnp.full((k, s.shape[1]), INT32_MIN, jnp.int32)
    for r in range(k):
        m = jnp.max(key, axis=0, keepdims=True)
        vals.append(_order_key(m))
        stacked = jnp.where(rowid == r, m, stacked)
        key = jnp.where(key == m, jnp.int32(INT32_MIN + r), key)
    rank = jnp.where(key < jnp.int32(INT32_MIN + k), key & jnp.int32(INT32_MAX), k).astype(F32)
    return vals, _order_key(stacked), rank


def _pair_candidates(t1, t1s, t2, t2s):
    K, sub = PEER_TOPK, 8
    rowid = lax.broadcasted_iota(jnp.int32, (sub, t1s.shape[1]), 0)
    pieces = [t1[0] + t2s]
    for a in range(1, sub):
        pieces.append(jnp.where(rowid < K // (a + 1), t1[a] + t2s[:sub], NEG_INF))
    pieces.append(t1s[sub:] + t2[0])
    return jnp.concatenate(pieces, axis=0)


def _peer_select_kernel(x_ref, g_ref, wq_ref, keys_ref, hnt_ref, n1_ref, e1_ref, r2_ref, e2_ref):
    K = PEER_TOPK
    hf = _rmsnorm_rows(x_ref[...], g_ref[...])
    hnt_ref[...] = _pack(hf.T.astype(BF16))
    q = _dot(hf.astype(BF16), wq_ref[...]).astype(BF16)
    half = PEER_DQ // 2
    for hd in range(PEER_HEADS):
        s1 = _dot_nt(keys_ref[hd, 0], q[:, hd * PEER_DQ:hd * PEER_DQ + half])
        s2 = _dot_nt(keys_ref[hd, 1], q[:, hd * PEER_DQ + half:(hd + 1) * PEER_DQ])
        t1, t1s, rank1 = _top_values(s1, K)
        t2, t2s, rank2 = _top_values(s2, K)
        cand = _pair_candidates(t1, t1s, t2, t2s)
        tau = _kth_value(cand, K)
        z = jnp.sum(jnp.where(cand >= tau, jnp.exp(cand - (t1[0] + t2[0])), 0.0), axis=0, keepdims=True)
        rank1_b = rank1.astype(BF16)
        n1 = jnp.zeros_like(rank1_b)
        for a in range(K):
            n_a = jnp.sum(jnp.where(t1[a] + t2s >= tau, 1.0, 0.0), axis=0, keepdims=True)
            n1 = jnp.where(rank1_b == float(a), jnp.broadcast_to(n_a, s1.shape).astype(BF16), n1)
        n1_ref[hd] = n1.astype(F32)
        e1_ref[hd] = jnp.where(rank1 < float(K), jnp.exp(s1 - t1[0]), 0.0)
        r2_ref[hd] = _pack(rank2.astype(BF16))
        e2_ref[hd] = _pack((jnp.exp(s2 - t2[0]) / z).astype(BF16))


def peer_select(x, g, wq, keys, *, tt):
    T, D = x.shape
    H, NK = PEER_HEADS, PEER_NKEYS
    const2 = lambda i: (0, 0)
    row_spec = pl.BlockSpec((H, NK, tt), lambda i: (0, 0, i))
    key_spec = pl.BlockSpec((H, NK // 2, tt), lambda i: (0, 0, i))
    return pl.pallas_call(
        _peer_select_kernel,
        grid=(T // tt,),
        in_specs=[
            pl.BlockSpec((tt, D), lambda i: (i, 0)),
            pl.BlockSpec((1, D), const2),
            pl.BlockSpec(wq.shape, const2),
            pl.BlockSpec(keys.shape, lambda i: (0, 0, 0, 0)),
        ],
        out_specs=[pl.BlockSpec((D // 2, tt), lambda i: (0, i)), row_spec, row_spec, key_spec, key_spec],
        out_shape=[
            jax.ShapeDtypeStruct((D // 2, T), U32),
            jax.ShapeDtypeStruct((H, NK, T), F32),
            jax.ShapeDtypeStruct((H, NK, T), F32),
            jax.ShapeDtypeStruct((H, NK // 2, T), U32),
            jax.ShapeDtypeStruct((H, NK // 2, T), U32),
        ],
        compiler_params=_params("parallel"),
        name="peer_select",
    )(x, g, wq, keys)


def _gelu(a):
    return a * 0.5 * (1.0 + lax.erf(a * (1.0 / math.sqrt(2.0))))


MXU_DIM = 256
STEP_GROUPS = (4, 4)


def _peer_dense_kernel(x_ref, hnt_ref, u_ref, vt_ref, n1_ref, e1_ref, r2_ref, e2_ref, g_ref, o_ref, acc,
                       *, final):
    j = pl.program_id(1)
    tt = x_ref.shape[0]
    assert sum(STEP_GROUPS) * PEER_NKEYS == 2 * u_ref.shape[0]
    first = [sum(STEP_GROUPS[:s]) for s in range(len(STEP_GROUPS) + 1)]

    @pl.when(j == 0)
    def _():
        acc[...] = jnp.zeros_like(acc)

    hnt = _unpack(hnt_ref[...])

    def scores(s):
        rows = slice(first[s] * PEER_NKEYS // 2, first[s + 1] * PEER_NKEYS // 2)
        return _dot(_unpack(u_ref[rows, :]), hnt)

    def gates(s, tok):
        out = [None] * STEP_GROUPS[s]
        for hd in range(PEER_HEADS):
            r2 = _unpack(r2_ref[hd, :, tok])
            e2 = _unpack(e2_ref[hd, :, tok])
            for g in range(STEP_GROUPS[s]):
                r = first[s] + g
                nb = jnp.broadcast_to(n1_ref[hd, r:r + 1, tok], (PEER_NKEYS, LANES)).astype(BF16)
                eb = jnp.broadcast_to(e1_ref[hd, r:r + 1, tok], (PEER_NKEYS, LANES)).astype(BF16)
                prod = e2 * eb
                term = jnp.where(r2 < nb, prod, jnp.zeros_like(prod))
                out[g] = term if out[g] is None else out[g] + term
        return jnp.concatenate(out, axis=0)

    def weights(s, a):
        cols = []
        for t0 in range(0, tt, LANES):
            tok = slice(t0, t0 + LANES)
            cols.append(gates(s, tok) * _gelu(a[:, tok]).astype(BF16))
        return jnp.concatenate(cols, axis=1)

    a_next = scores(0)
    for s in range(len(STEP_GROUPS)):
        a_cur = a_next
        if s + 1 < len(STEP_GROUPS):
            a_next = scores(s + 1)
        w = weights(s, a_cur)
        cols = slice(first[s] * PEER_NKEYS, first[s + 1] * PEER_NKEYS)
        acc[...] += _dot(_unpack(vt_ref[:, cols]), w)

    @pl.when(j == pl.num_programs(1) - 1)
    def _():
        o = x_ref[...] + acc[...].T
        o_ref[...] = _rmsnorm_rows(o, g_ref[...]) if final else o


def peer_dense(x, hnt, u, vt, n1, e1, r2, e2, g, *, tt, ec, final):
    T, D = x.shape
    E = 2 * u.shape[0]
    H, NK = PEER_HEADS, PEER_NKEYS
    row_spec = pl.BlockSpec((H, ec // NK, tt), lambda i, j: (0, j, i))
    key_spec = pl.BlockSpec((H, NK // 2, tt), lambda i, j: (0, 0, i))
    return pl.pallas_call(
        functools.partial(_peer_dense_kernel, final=final),
        grid=(T // tt, E // ec),
        in_specs=[
            pl.BlockSpec((tt, D), lambda i, j: (i, 0)),
            pl.BlockSpec((D // 2, tt), lambda i, j: (0, i)),
            pl.BlockSpec((ec // 2, D), lambda i, j: (j, 0)),
            pl.BlockSpec((D // 2, ec), lambda i, j: (0, j)),
            row_spec, row_spec, key_spec, key_spec,
            pl.BlockSpec((1, D), lambda i, j: (0, 0)),
        ],
        out_specs=pl.BlockSpec((tt, D), lambda i, j: (i, 0)),
        out_shape=jax.ShapeDtypeStruct((T, D), F32),
        scratch_shapes=[pltpu.VMEM((D, tt), F32)],
        compiler_params=_params("parallel", "arbitrary"),
        name="peer_dense",
    )(x, hnt, u, vt, n1, e1, r2, e2, g)


def _pack_rows_kernel(w_ref, o_ref):
    o_ref[...] = _pack(w_ref[...].astype(BF16))


def pack_rows(w, layer, *, tr):
    _, R, C = w.shape
    return pl.pallas_call(
        _pack_rows_kernel,
        grid=(R // tr,),
        in_specs=[pl.BlockSpec((None, tr, C), lambda i: (layer, i, 0))],
        out_specs=pl.BlockSpec((tr // 2, C), lambda i: (i, 0)),
        out_shape=jax.ShapeDtypeStruct((R // 2, C), U32),
        compiler_params=_params("parallel"),
        name="pack_rows",
    )(w)


def _pack_transposed_kernel(w_ref, o_ref):
    o_ref[...] = _pack(w_ref[...].T.astype(BF16))


def pack_transposed(w, layer, *, tr):
    _, R, C = w.shape
    return pl.pallas_call(
        _pack_transposed_kernel,
        grid=(R // tr,),
        in_specs=[pl.BlockSpec((None, tr, C), lambda i: (layer, i, 0))],
        out_specs=pl.BlockSpec((C // 2, tr), lambda i: (0, i)),
        out_shape=jax.ShapeDtypeStruct((C // 2, R), U32),
        compiler_params=_params("parallel"),
        name="pack_transposed",
    )(w)


def _rope_tables(S):
    d = RET_HEAD_DIM
    inv = ROPE_BASE ** (-jnp.arange(0, d, 2, dtype=F32) / d)
    ang = jnp.arange(S, dtype=F32)[:, None] * inv[None, :]
    cos, sin = jnp.cos(ang), jnp.sin(ang)
    return jnp.concatenate([cos, cos], axis=1), jnp.concatenate([-sin, sin], axis=1)


def _tile(n, want):
    t = min(n, want)
    assert n % t == 0, (n, t)
    return t


def kernel(x, mem, mix_norm, w_in, sb_norm, ret_norm, w_out, xa_norm, mem_norm, xa_wq, xa_wkv, xa_wo,
           peer_norm, peer_wq, peer_keys, peer_u, peer_v, final_norm_g):
    B, S, D = x.shape
    M = mem.shape[1]
    T = B * S
    depth = w_in.shape[0]
    assert D == D_MODEL and S % SB_BLOCK == 0 and w_in.shape[2] == IN_COLS

    cos, sin = _rope_tables(S)
    log_gamma = jnp.log1p(-jnp.power(2.0, -5.0 - jnp.arange(RET_HEADS, dtype=F32)))
    log_gamma = jnp.broadcast_to(log_gamma[:, None, None], (RET_HEADS, 1, LANES))
    row = lambda v: v.reshape(1, -1).astype(F32)
    tm = _tile(T, 512)

    xt = x.reshape(T, D)
    mem2 = mem.reshape(B * M, D)
    for l in range(depth):
        proj = norm_matmul(xt, row(mix_norm[l]), w_in[l].astype(BF16), tm=tm, tn=IN_COLS // 7)
        proj = proj.reshape(B, S, IN_COLS)
        sb_o = sb_attention(proj, row(sb_norm[l]), rows=4 if B % 4 == 0 else 2 if B % 2 == 0 else 1)
        ret_o = retention(proj, row(ret_norm[l]), cos, sin, log_gamma)
        w_o = w_out[l].astype(BF16)
        kv = norm_matmul(mem2, row(mem_norm[l]), xa_wkv[l].astype(BF16), tm=_tile(B * M, 512), tn=1024)
        xt = cross_attn(xt.reshape(B, S, D), sb_o, ret_o, w_o[:SB_WIDTH], w_o[SB_WIDTH:],
                        row(xa_norm[l]), xa_wq[l].astype(BF16), kv.reshape(B, M, 2 * D),
                        xa_wo[l].astype(BF16), tq=_tile(S, 512)).reshape(T, D)

        hn, n1, e1, r2, e2 = peer_select(xt, row(peer_norm[l]), peer_wq[l].astype(BF16),
                                         peer_keys[l].astype(BF16), tt=_tile(T, 256))
        xt = peer_dense(xt, hn, pack_rows(peer_u, l, tr=1024), pack_transposed(peer_v, l, tr=512),
                        n1, e1, r2, e2, row(final_norm_g), tt=_tile(T, 1024), ec=1024,
                        final=l == depth - 1)
    return xt.reshape(B, S, D)
```
